```python
import math
import jax, jax.numpy as jnp
from jax import lax
import numpy as np

D_MODEL = 1024
BATCH = 16
SEQ = 2048
DEPTH = 2

MLSTM_HEADS = 4
MLSTM_DH = 128
MLSTM_W = MLSTM_HEADS * MLSTM_DH
MLSTM_CHUNK = 128
CONV_W = 5
N_MGATES = 4 * MLSTM_HEADS

DIFF_HEADS = 4
DIFF_DH = 64
DIFF_DV = 2 * DIFF_DH
DIFF_QK_W = DIFF_HEADS * 2 * DIFF_DH
DIFF_W = DIFF_HEADS * DIFF_DV
Q_BLOCK = 128

OFF_MQ = 0
OFF_MK = OFF_MQ + MLSTM_W
OFF_MV = OFF_MK + MLSTM_W
OFF_MO = OFF_MV + MLSTM_W
OFF_MG = OFF_MO + MLSTM_W
OFF_DQ = OFF_MG + N_MGATES
OFF_DK = OFF_DQ + DIFF_QK_W
OFF_DV = OFF_DK + DIFF_QK_W
OFF_GATE = OFF_DV + DIFF_W
N_BRANCH = 2
IN_COLS = OFF_GATE + N_BRANCH * D_MODEL

N_EXPERTS = 16
EC_CAPACITY = 2
D_FF_EXPERT = 2 * D_MODEL

EPS = 1e-6

kernel_name = "hybrid_mlstm_diffattn_ecmoe_encoder"


def rms_norm(x, g):
    xf = x.astype(jnp.float32)
    xf = xf * lax.rsqrt(jnp.mean(xf * xf, axis=-1, keepdims=True) + EPS)
    return (xf * g.astype(jnp.float32)).astype(x.dtype)


def centred_dwconv(x, w, b):
    c = x.shape[-1]
    pad = CONV_W // 2
    y = lax.conv_general_dilated(x, w[:, None, :].astype(x.dtype), window_strides=(1,),
                                 padding=[(pad, pad)], dimension_numbers=('NWC', 'WIO', 'NWC'),
                                 feature_group_count=c)
    return y + b.astype(x.dtype)


def mlstm_chunkwise(q, k, v, i_pre, f_pre):
    B, H, S, dk = q.shape
    dv = v.shape[-1]
    L = MLSTM_CHUNK
    NC = S // L
    q = q.astype(jnp.float32).reshape(B, H, NC, L, dk)
    k = (k.astype(jnp.float32) * dk ** -0.5).reshape(B, H, NC, L, dk)
    v = v.astype(jnp.float32).reshape(B, H, NC, L, dv)
    log_f = jax.nn.log_sigmoid(f_pre).reshape(B, H, NC, L)
    log_i = i_pre.reshape(B, H, NC, L)
    b = jnp.cumsum(log_f, axis=-1)
    g = b[..., -1]
    a = g[..., None] - b + log_i
    a_max = a.max(-1)
    w = jnp.exp(a - a_max[..., None])
    C_loc = jnp.einsum('bhclv,bhclk->bhcvk', v * w[..., None], k)
    n_loc = jnp.einsum('bhcl,bhclk->bhck', w, k)

    def step(carry, inp):
        C, n, m = carry
        C_l, n_l, am, gc = inp
        m_new = jnp.maximum(gc + m, am)
        sp = jnp.exp(gc + m - m_new)
        sl = jnp.exp(am - m_new)
        C_new = sp[..., None, None] * C + sl[..., None, None] * C_l
        n_new = sp[..., None] * n + sl[..., None] * n_l
        return (C_new, n_new, m_new), (C, n, m)

    xs = (jnp.moveaxis(C_loc, 2, 0), jnp.moveaxis(n_loc, 2, 0),
          jnp.moveaxis(a_max, 2, 0), jnp.moveaxis(g, 2, 0))
    init = (jnp.zeros((B, H, dv, dk), jnp.float32), jnp.zeros((B, H, dk), jnp.float32),
            jnp.zeros((B, H), jnp.float32))
    _, (C_in, n_in, m_in) = lax.scan(step, init, xs)
    C_in = jnp.moveaxis(C_in, 0, 2)
    n_in = jnp.moveaxis(n_in, 0, 2)
    m_in = jnp.moveaxis(m_in, 0, 2)

    lower = jnp.tril(jnp.ones((L, L), dtype=bool))
    D = b[..., :, None] - b[..., None, :] + log_i[..., None, :]
    D = jnp.where(lower, D, -jnp.inf)
    inter = b + m_in[..., None]
    m_t = jnp.maximum(inter, D.max(-1))
    P = jnp.exp(D - m_t[..., None])
    W = jnp.einsum('bhctd,bhcsd->bhcts', q, k) * P
    s_in = jnp.exp(inter - m_t)
    num = (jnp.einsum('bhcts,bhcsv->bhctv', W, v)
           + s_in[..., None] * jnp.einsum('bhcvk,bhctk->bhctv', C_in, q))
    den = W.sum(-1) + s_in * jnp.einsum('bhck,bhctk->bhct', n_in, q)
    h = num / jnp.maximum(jnp.abs(den), jnp.exp(-m_t))[..., None]
    return h.reshape(B, H, S, dv)


def mlstm_branch(z, conv_w, conv_b, b_mgate, norm_g):
    B, S, _ = z.shape
    qk = jax.nn.silu(centred_dwconv(z[..., OFF_MQ:OFF_MV], conv_w, conv_b))
    to_heads = lambda t: t.reshape(B, S, MLSTM_HEADS, MLSTM_DH).transpose(0, 2, 1, 3)
    q = to_heads(qk[..., :MLSTM_W])
    k = to_heads(qk[..., MLSTM_W:])
    v = to_heads(z[..., OFF_MV:OFF_MO])
    o = z[..., OFF_MO:OFF_MG]
    gates = z[..., OFF_MG:OFF_MG + N_MGATES].astype(jnp.float32) + b_mgate.astype(jnp.float32)
    gates = gates.reshape(B, S, 2, 2, MLSTM_HEADS).transpose(2, 3, 0, 4, 1)
    h_fwd = mlstm_chunkwise(q, k, v, gates[0, 0], gates[0, 1])
    flip = lambda t: jnp.flip(t, axis=2)
    h_bwd = flip(mlstm_chunkwise(flip(q), flip(k), flip(v),
                                 jnp.flip(gates[1, 0], -1), jnp.flip(gates[1, 1], -1)))
    h = h_fwd + h_bwd
    mu = h.mean(-1, keepdims=True)
    var = jnp.mean((h - mu) ** 2, axis=-1, keepdims=True)
    h = ((h - mu) * lax.rsqrt(var + EPS)).transpose(0, 2, 1, 3).reshape(B, S, MLSTM_W)
    h = h * norm_g.astype(jnp.float32) * jax.nn.sigmoid(o.astype(jnp.float32))
    return h.astype(z.dtype)


def diff_attention_branch(z, lam, norm_g, lam_init):
    B, S, _ = z.shape
    H = DIFF_HEADS
    q = z[..., OFF_DQ:OFF_DK].reshape(B, S, H, 2, DIFF_DH).transpose(0, 2, 3, 1, 4)
    k = z[..., OFF_DK:OFF_DV].reshape(B, S, H, 2, DIFF_DH).transpose(0, 2, 3, 1, 4)
    v = z[..., OFF_DV:OFF_GATE].reshape(B, S, H, DIFF_DV).transpose(0, 2, 1, 3)
    lamf = lam.astype(jnp.float32)
    lam_full = (jnp.exp(jnp.sum(lamf[0] * lamf[1])) - jnp.exp(jnp.sum(lamf[2] * lamf[3]))
                + lam_init)
    slopes = jnp.exp2(-8.0 * jnp.arange(1, H + 1, dtype=jnp.float32) / H)
    scale = DIFF_DH ** -0.5
    NB = S // Q_BLOCK
    qb = q.reshape(B, H, 2, NB, Q_BLOCK, DIFF_DH).transpose(3, 0, 1, 2, 4, 5)
    pos = jnp.arange(S, dtype=jnp.int32)

    def block(args):
        q_blk, start = args
        s = jnp.einsum('bhiqd,bhikd->bhiqk', q_blk, k).astype(jnp.float32) * scale
        tq = start + jnp.arange(Q_BLOCK, dtype=jnp.int32)
        dist = jnp.abs(tq[:, None] - pos[None, :]).astype(jnp.float32)
        s = s - (slopes[:, None, None] * dist)[None, :, None]
        p = jax.nn.softmax(s, axis=-1)
        a = p[:, :, 0] - lam_full * p[:, :, 1]
        return jnp.einsum('bhqk,bhkv->bhqv', a.astype(v.dtype), v)

    starts = jnp.arange(NB, dtype=jnp.int32) * Q_BLOCK
    o = lax.map(block, (qb, starts))
    o = o.transpose(1, 2, 0, 3, 4).reshape(B, H, S, DIFF_DV).astype(jnp.float32)
    o = o * lax.rsqrt(jnp.mean(o * o, axis=-1, keepdims=True) + EPS)
    o = o * norm_g.astype(jnp.float32) * (1.0 - lam_init)
    return o.transpose(0, 2, 1, 3).reshape(B, S, DIFF_W).astype(z.dtype)


def expert_choice_ffn(h, w_router, w_gate, w_up, w_down):
    B, S, _ = h.shape
    cap = EC_CAPACITY * S // N_EXPERTS
    aff = jax.nn.softmax(jnp.einsum('bsd,de->bse', h, w_router).astype(jnp.float32), axis=-1)
    g, idx = lax.top_k(aff.transpose(0, 2, 1), cap)
    bidx = jnp.arange(B)[:, None, None]
    xe = h[bidx, idx]
    hid = (jax.nn.silu(jnp.einsum('becd,edf->becf', xe, w_gate))
           * jnp.einsum('becd,edf->becf', xe, w_up))
    ye = jnp.einsum('becf,efd->becd', hid, w_down) * g[..., None].astype(h.dtype)
    return jnp.zeros_like(h).at[bidx, idx].add(ye)


def setup_inputs(seed: int = 0) -> dict:
    key = jax.random.key(seed)
    ks = jax.random.split(key, 20)
    nrm = lambda k, shape, s: jax.random.normal(k, shape, jnp.float32) * s
    i_bias = nrm(ks[3], (DEPTH, 2, MLSTM_HEADS), 0.1)
    f_bias = jnp.linspace(3.0, 6.0, MLSTM_HEADS, dtype=jnp.float32) + nrm(ks[4], (DEPTH, 2, MLSTM_HEADS), 0.1)
    b_mgate = jnp.stack([i_bias, f_bias], axis=2).reshape(DEPTH, N_MGATES)
    return {
        "x": nrm(ks[0], (BATCH, SEQ, D_MODEL), 1.0),
        "norm_mix_g": 1.0 + nrm(ks[1], (DEPTH, D_MODEL), 0.01),
        "w_in": nrm(ks[2], (DEPTH, D_MODEL, IN_COLS), D_MODEL ** -0.5),
        "b_mgate": b_mgate,
        "conv_w": nrm(ks[5], (DEPTH, CONV_W, 2 * MLSTM_W), CONV_W ** -0.5),
        "conv_b": nrm(ks[6], (DEPTH, 2 * MLSTM_W), 0.01),
        "mlstm_norm_g": 1.0 + nrm(ks[7], (DEPTH, MLSTM_W), 0.01),
        "diff_lam": nrm(ks[8], (DEPTH, 4, DIFF_DH), 0.1),
        "diff_norm_g": 1.0 + nrm(ks[9], (DEPTH, DIFF_DV), 0.01),
        "w_up_a": nrm(ks[10], (DEPTH, MLSTM_W, D_MODEL), MLSTM_W ** -0.5),
        "w_up_b": nrm(ks[11], (DEPTH, DIFF_W, D_MODEL), DIFF_W ** -0.5),
        "w_out": nrm(ks[12], (DEPTH, D_MODEL, D_MODEL), D_MODEL ** -0.5),
        "norm_ffn_g": 1.0 + nrm(ks[13], (DEPTH, D_MODEL), 0.01),
        "w_router": nrm(ks[14], (DEPTH, D_MODEL, N_EXPERTS), D_MODEL ** -0.5),
        "w_gate_e": nrm(ks[15], (DEPTH, N_EXPERTS, D_MODEL, D_FF_EXPERT), D_MODEL ** -0.5),
        "w_up_e": nrm(ks[16], (DEPTH, N_EXPERTS, D_MODEL, D_FF_EXPERT), D_MODEL ** -0.5),
        "w_down_e": nrm(ks[17], (DEPTH, N_EXPERTS, D_FF_EXPERT, D_MODEL), D_FF_EXPERT ** -0.5),
        "norm_f_g": 1.0 + nrm(ks[18], (D_MODEL,), 0.01),
    }


def reference(x, norm_mix_g, w_in, b_mgate, conv_w, conv_b, mlstm_norm_g, diff_lam, diff_norm_g,
              w_up_a, w_up_b, w_out, norm_ffn_g, w_router, w_gate_e, w_up_e, w_down_e, norm_f_g):
    for l in range(DEPTH):
        lam_init = 0.8 - 0.6 * math.exp(-0.3 * l)
        h = rms_norm(x, norm_mix_g[l])
        z = jnp.einsum('bsd,dc->bsc', h, w_in[l])
        y_a = jnp.einsum('bsw,wd->bsd', mlstm_branch(z, conv_w[l], conv_b[l], b_mgate[l], mlstm_norm_g[l]), w_up_a[l])
        y_b = jnp.einsum('bsw,wd->bsd', diff_attention_branch(z, diff_lam[l], diff_norm_g[l], lam_init), w_up_b[l])
        g_a = jax.nn.sigmoid(z[..., OFF_GATE:OFF_GATE + D_MODEL])
        g_b = jax.nn.sigmoid(z[..., OFF_GATE + D_MODEL:])
        x = x + jnp.einsum('bsd,de->bse', g_a * y_a + g_b * y_b, w_out[l])
        h2 = rms_norm(x, norm_ffn_g[l])
        x = x + expert_choice_ffn(h2, w_router[l], w_gate_e[l], w_up_e[l], w_down_e[l])
    return rms_norm(x, norm_f_g)
```

```python
import functools
import math

import jax
import jax.numpy as jnp
from jax import lax
from jax.experimental import pallas as pl
from jax.experimental.pallas import tpu as pltpu

F32 = jnp.float32
BF16 = jnp.bfloat16
EPS = 1e-6

LANES = 128
VMEM_LIMIT_BYTES = 56 * 1024 * 1024

M_HEADS = 4
M_DH = 128
M_W = M_HEADS * M_DH
CHUNK = 128
CONV_TAPS = 5
N_MGATES = 4 * M_HEADS
D_HEADS = 4
D_DH = 64
D_DV = 2 * D_DH
D_W = D_HEADS * D_DV
N_EXPERTS = 16
EC_CAPACITY = 2

HI = lax.Precision.HIGHEST
NT_DIMS = (((1,), (1,)), ((), ()))
TN_DIMS = (((0,), (0,)), ((), ()))


def _params(n_axes):
    return pltpu.CompilerParams(dimension_semantics=("arbitrary",) * n_axes,
                                vmem_limit_bytes=VMEM_LIMIT_BYTES)


def _sigmoid(v):
    return 1.0 / (1.0 + jnp.exp(-v))


def _log_sigmoid(v):
    return jnp.minimum(v, 0.0) - jnp.log1p(jnp.exp(-jnp.abs(v)))


def _rms(v, g):
    return v * lax.rsqrt(jnp.mean(v * v, axis=-1, keepdims=True) + EPS) * g


def _inproj_kernel(x_ref, g_ref, wqk_ref, wv_ref, wo_ref, wg_ref, wdk_ref, wgate_ref, wt_ref,
                   zqk_ref, zv_ref, so_ref, gc_ref, dk_ref, sgate_ref, dqt_ref, dvt_ref, gr_ref):
    hb = _rms(x_ref[...], g_ref[...]).astype(BF16)

    def proj(w_ref):
        return jnp.dot(hb, w_ref[...], preferred_element_type=F32)

    zqk_ref[...] = proj(wqk_ref).astype(BF16)
    zv_ref[...] = proj(wv_ref).astype(BF16)
    so_ref[...] = _sigmoid(proj(wo_ref)).astype(BF16)
    gc_ref[...] = proj(wg_ref)
    dk_ref[...] = proj(wdk_ref).astype(BF16)
    sgate_ref[...] = _sigmoid(proj(wgate_ref)).astype(BF16)
    zt = lax.dot_general(wt_ref[...], hb, NT_DIMS, preferred_element_type=F32)
    dqt_ref[...] = zt[:D_W].astype(BF16)
    dvt_ref[...] = zt[D_W:2 * D_W].astype(BF16)
    gr_ref[...] = zt[2 * D_W:]


def _in_projection(x2, norm_g, w_in):
    t, d = x2.shape
    tm = min(512, t)
    o_mq, o_mv, o_mo, o_mg = 0, 2 * M_W, 3 * M_W, 4 * M_W
    o_dq = o_mg + N_MGATES
    o_dk = o_dq + D_W
    o_dv = o_dk + D_W
    o_gate = o_dv + D_W
    w_g = w_in[:, o_mg:o_dq]
    wqk = w_in[:, o_mq:o_mv].astype(BF16)
    wv = w_in[:, o_mv:o_mo].astype(BF16)
    wo = w_in[:, o_mo:o_mg].astype(BF16)
    wg = jnp.pad(w_g, ((0, 0), (0, LANES - N_MGATES))).astype(BF16)
    wdk = w_in[:, o_dk:o_dv].astype(BF16)
    wgate = w_in[:, o_gate:].astype(BF16)
    wt = jnp.concatenate([
        (w_in[:, o_dq:o_dk] * (D_DH ** -0.5)).T,
        w_in[:, o_dv:o_gate].T,
        jnp.pad(w_g, ((0, 0), (0, LANES - N_MGATES))).T], axis=0).astype(BF16)
    weights = (wqk, wv, wo, wg, wdk, wgate, wt)

    def full(a):
        return pl.BlockSpec(a.shape, lambda i: (0, 0))

    def rows(n):
        return pl.BlockSpec((tm, n), lambda i: (i, 0))

    def cols(n):
        return pl.BlockSpec((n, tm), lambda i: (0, i))

    out_shape = (
        jax.ShapeDtypeStruct((t, 2 * M_W), BF16),
        jax.ShapeDtypeStruct((t, M_W), BF16),
        jax.ShapeDtypeStruct((t, M_W), BF16),
        jax.ShapeDtypeStruct((t, LANES), F32),
        jax.ShapeDtypeStruct((t, D_W), BF16),
        jax.ShapeDtypeStruct((t, 2 * d), BF16),
        jax.ShapeDtypeStruct((D_W, t), BF16),
        jax.ShapeDtypeStruct((D_W, t), BF16),
        jax.ShapeDtypeStruct((LANES, t), F32),
    )
    out_specs = (rows(2 * M_W), rows(M_W), rows(M_W), rows(LANES), rows(D_W), rows(2 * d),
                 cols(D_W), cols(D_W), cols(LANES))
    return pl.pallas_call(
        _inproj_kernel,
        grid=(t // tm,),
        in_specs=[rows(d), full(norm_g)] + [full(w) for w in weights],
        out_specs=out_specs,
        out_shape=out_shape,
        compiler_params=_params(1),
        name="inproj",
    )(x2, norm_g, *weights)


def _mlstm_kernel(zq_ref, zk_ref, zv_ref, so_ref, gc_ref, gr_ref, cwq_ref, cwk_ref, cbq_ref, cbk_ref,
                  bgr_ref, bgc_ref, ng_ref,
                  out_ref,
                  xpad_ref, q_ref, k_ref, gcs_ref, cumc_ref, grs_ref, cumr_ref, hdir_ref):
    head = pl.program_id(1)
    s = zv_ref.shape[0]
    nc = s // CHUNK
    L = CHUNK
    pad = 8
    rc = min(256, s)

    ri = lax.broadcasted_iota(jnp.int32, (L, L), 0)
    ci = lax.broadcasted_iota(jnp.int32, (L, L), 1)
    lower = (ci <= ri)
    upper = (ci >= ri)

    @pl.when(head == 0)
    def _():
        tril = lower.astype(F32)
        triu = upper.astype(F32)
        grow = gr_ref[...] + bgc_ref[...]
        grs_ref[...] = grow
        lsr = _log_sigmoid(grow).reshape(nc * N_MGATES, L)
        cumr_ref[0] = jnp.dot(lsr, triu, precision=HI, preferred_element_type=F32).reshape(nc, N_MGATES, L)
        cumr_ref[1] = jnp.dot(lsr, tril, precision=HI, preferred_element_type=F32).reshape(nc, N_MGATES, L)

        def cum_cols(c, carry):
            r = pl.ds(pl.multiple_of(c * L, L), L)
            gcol = gc_ref[r, :N_MGATES] + bgr_ref[...]
            gcs_ref[r, :] = gcol
            lsc = _log_sigmoid(gcol)
            cumc_ref[0, r, :] = jnp.dot(tril, lsc, precision=HI, preferred_element_type=F32)
            cumc_ref[1, r, :] = jnp.dot(triu, lsc, precision=HI, preferred_element_type=F32)
            return carry

        lax.fori_loop(0, nc, cum_cols, 0)

    xpad_ref[0:pad, :] = jnp.zeros((pad, LANES), F32)
    xpad_ref[pad + s:2 * pad + s, :] = jnp.zeros((pad, LANES), F32)
    half = CONV_TAPS // 2
    for z_ref, cw_ref, cb_ref, dst_ref, scale in ((zq_ref, cwq_ref, cbq_ref, q_ref, None),
                                                  (zk_ref, cwk_ref, cbk_ref, k_ref, M_DH ** -0.5)):
        for c0 in range(0, s, rc):
            xpad_ref[pad + c0:pad + c0 + rc, :] = z_ref[c0:c0 + rc, :].astype(F32)
        for c0 in range(0, s, rc):
            acc = cb_ref[...]
            for tap in range(CONV_TAPS):
                o = pad - half + tap + c0
                acc = acc + cw_ref[tap:tap + 1, :] * xpad_ref[o:o + rc, :]
            y = acc * _sigmoid(acc)
            if scale is not None:
                y = y * scale
            dst_ref[c0:c0 + rc, :] = y.astype(BF16)

    neg_inf = jnp.float32(-jnp.inf)

    def chunk_step(h, d, c, state):
        ct_in, n_in, m_in = state
        r = pl.ds(pl.multiple_of(c * L, L), L)
        fcol = d * 2 * M_HEADS + M_HEADS + h
        icol = d * 2 * M_HEADS + h
        q = q_ref[r, :]
        k = k_ref[r, :]
        v = zv_ref[r, :]
        b_col = cumc_ref[d, r, fcol:fcol + 1]
        u_col = gcs_ref[r, icol:icol + 1] - b_col
        b_row = cumr_ref[d, c, fcol:fcol + 1, :]
        u_row = grs_ref[c, icol:icol + 1, :] - b_row
        g = b_row[:, L - 1:L] if d == 0 else b_row[:, 0:1]
        a_max = g + jnp.max(u_row, axis=1, keepdims=True)
        w_col = jnp.exp(g + u_col - a_max)
        vw = (v.astype(F32) * w_col).astype(BF16)
        ct_loc = lax.dot_general(k, vw, TN_DIMS, preferred_element_type=F32)
        n_loc = jnp.sum(k.astype(F32) * w_col, axis=0, keepdims=True)
        dm = jnp.where(lower if d == 0 else upper, b_col + u_row, neg_inf)
        inter = b_col + m_in
        m_t = jnp.maximum(inter, jnp.max(dm, axis=1, keepdims=True))
        p = jnp.exp(dm - m_t)
        w = lax.dot_general(q, k, NT_DIMS, preferred_element_type=F32) * p
        s_in = jnp.exp(inter - m_t)
        num = (jnp.dot(w.astype(BF16), v, preferred_element_type=F32)
               + s_in * jnp.dot(q, ct_in.astype(BF16), preferred_element_type=F32))
        den = (jnp.sum(w, axis=1, keepdims=True)
               + s_in * jnp.sum(q.astype(F32) * n_in, axis=1, keepdims=True))
        hdir_ref[d, r, :] = num / jnp.maximum(jnp.abs(den), jnp.exp(-m_t))
        m_new = jnp.maximum(g + m_in, a_max)
        sp = jnp.exp(g + m_in - m_new)
        sl = jnp.exp(a_max - m_new)
        return (sp * ct_in + sl * ct_loc, sp * n_in + sl * n_loc, m_new)

    for h in range(M_HEADS):
        @pl.when(head == h)
        def _(h=h):
            def body(c, carry):
                sf, sb = carry
                sf = chunk_step(h, 0, c, sf)
                sb = chunk_step(h, 1, nc - 1 - c, sb)
                return (sf, sb)

            zero = (jnp.zeros((M_DH, M_DH), F32), jnp.zeros((1, M_DH), F32), jnp.zeros((1, 1), F32))
            lax.fori_loop(0, nc, body, (zero, zero))

    def finish(c, carry):
        r = pl.ds(pl.multiple_of(c * rc, rc), rc)
        hs = hdir_ref[0, r, :] + hdir_ref[1, r, :]
        mu = jnp.mean(hs, axis=-1, keepdims=True)
        hc = hs - mu
        var = jnp.mean(hc * hc, axis=-1, keepdims=True)
        hn = hc * lax.rsqrt(var + EPS) * ng_ref[...] * so_ref[r, :].astype(F32)
        out_ref[r, :] = hn.astype(BF16)
        return carry

    lax.fori_loop(0, s // rc, finish, 0)


def _mlstm_branch(zqk, zv, so, gcol, grow, conv_w, conv_b, b_mgate, norm_g):
    b, s, _ = zv.shape
    nc = s // CHUNK
    bgr = b_mgate.reshape(1, N_MGATES)
    bgc = b_mgate.reshape(N_MGATES, 1)
    cb = conv_b.reshape(1, -1)
    ng = norm_g.reshape(1, -1)

    def head_block(off):
        return pl.BlockSpec((None, s, M_DH), lambda i, h: (i, 0, h + off))

    def head_cols(rows, off):
        return pl.BlockSpec((rows, M_DH), lambda i, h: (0, h + off))

    def full(a):
        return pl.BlockSpec(a.shape, lambda i, h: (0,) * a.ndim)

    return pl.pallas_call(
        _mlstm_kernel,
        grid=(b, M_HEADS),
        in_specs=[
            head_block(0), head_block(M_HEADS),
            head_block(0), head_block(0),
            pl.BlockSpec((None, s, LANES), lambda i, h: (i, 0, 0)),
            pl.BlockSpec((None, nc, N_MGATES, CHUNK), lambda i, h: (i, 0, 0, 0)),
            head_cols(CONV_TAPS, 0), head_cols(CONV_TAPS, M_HEADS),
            head_cols(1, 0), head_cols(1, M_HEADS),
            full(bgr), full(bgc), head_cols(1, 0),
        ],
        out_specs=head_block(0),
        out_shape=jax.ShapeDtypeStruct((b, s, M_W), BF16),
        scratch_shapes=[
            pltpu.VMEM((s + 16, LANES), F32),
            pltpu.VMEM((s, M_DH), BF16),
            pltpu.VMEM((s, M_DH), BF16),
            pltpu.VMEM((s, N_MGATES), F32),
            pltpu.VMEM((2, s, N_MGATES), F32),
            pltpu.VMEM((nc, N_MGATES, CHUNK), F32),
            pltpu.VMEM((2, nc, N_MGATES, CHUNK), F32),
            pltpu.VMEM((2, s, M_DH), F32),
        ],
        compiler_params=_params(2),
        name="mlstm",
    )(zqk, zqk, zv, so, gcol, grow, conv_w, conv_w, cb, cb, bgr, bgc, ng)


def _diff_kernel(lam_ref, k_ref, qt_ref, vt_ref, ng_ref, out_ref, bias_ref, s_ref, a_ref,
                 *, tq, tk, lam_init):
    h = pl.program_id(0)
    qi = pl.program_id(1)
    b = pl.program_id(2)
    s = k_ref.shape[0]
    nk = s // tk

    @pl.when(b == 0)
    def _():
        slope = jnp.float32(2.0 ** (-8.0 * D_HEADS / D_HEADS))
        for i in range(D_HEADS - 1):
            slope = jnp.where(h == i, jnp.float32(2.0 ** (-8.0 * (i + 1) / D_HEADS)), slope)
        kpos = lax.broadcasted_iota(jnp.int32, (s, tq), 0)
        qpos = lax.broadcasted_iota(jnp.int32, (s, tq), 1) + qi * tq
        bias_ref[...] = slope * jnp.abs(qpos - kpos).astype(F32)

    lam = lam_ref[...]
    lam_full = (jnp.exp(jnp.sum(lam[0:1] * lam[1:2], axis=1, keepdims=True))
                - jnp.exp(jnp.sum(lam[2:3] * lam[3:4], axis=1, keepdims=True)) + lam_init)

    qt = qt_ref[...].astype(F32)
    rows = lax.broadcasted_iota(jnp.int32, qt.shape, 0)
    qblk = jnp.concatenate([jnp.where(rows < D_DH, qt, 0.0), jnp.where(rows >= D_DH, qt, 0.0)],
                           axis=1).astype(BF16)

    def scores(c, m):
        r = pl.ds(pl.multiple_of(c * tk, tk), tk)
        sc = jnp.dot(k_ref[r, :], qblk, preferred_element_type=F32)
        bias = bias_ref[r, :]
        sc = sc - jnp.concatenate([bias, bias], axis=1)
        s_ref[r, :] = sc
        return jnp.maximum(m, jnp.max(sc, axis=0, keepdims=True))

    m = lax.fori_loop(0, nk, scores, jnp.full((1, 2 * tq), -jnp.inf, F32))

    def exps(c, l):
        r = pl.ds(pl.multiple_of(c * tk, tk), tk)
        p = jnp.exp(s_ref[r, :] - m)
        s_ref[r, :] = p
        return l + jnp.sum(p, axis=0, keepdims=True)

    l = lax.fori_loop(0, nk, exps, jnp.zeros((1, 2 * tq), F32))
    r0 = 1.0 / l[:, :tq]
    r1 = lam_full / l[:, tq:]

    def combine(c, carry):
        r = pl.ds(pl.multiple_of(c * tk, tk), tk)
        p = s_ref[r, :]
        a_ref[r, :] = (p[:, :tq] * r0 - p[:, tq:] * r1).astype(BF16)
        return carry

    lax.fori_loop(0, nk, combine, 0)
    ot = jnp.dot(vt_ref[...], a_ref[...], preferred_element_type=F32)
    ot = ot * lax.rsqrt(jnp.mean(ot * ot, axis=0, keepdims=True) + EPS)
    ot = ot * ng_ref[...] * (1.0 - lam_init)
    out_ref[...] = ot.T.astype(BF16)


def _diff_attention(dk, dqt, dvt, lam, norm_g, lam_init, b, s):
    tq = min(256, s)
    tk = min(256, s)
    nq = s // tq
    ng = norm_g.reshape(D_DV, 1)
    kern = functools.partial(_diff_kernel, tq=tq, tk=tk, lam_init=lam_init)
    return pl.pallas_call(
        kern,
        grid=(D_HEADS, nq, b),
        in_specs=[
            pl.BlockSpec(lam.shape, lambda h, q, i: (0, 0)),
            pl.BlockSpec((None, s, D_DV), lambda h, q, i: (i, 0, h)),
            pl.BlockSpec((D_DV, tq), lambda h, q, i: (h, i * nq + q)),
            pl.BlockSpec((D_DV, s), lambda h, q, i: (h, i)),
            pl.BlockSpec(ng.shape, lambda h, q, i: (0, 0)),
        ],
        out_specs=pl.BlockSpec((None, tq, D_DV), lambda h, q, i: (i, q, h)),
        out_shape=jax.ShapeDtypeStruct((b, s, D_W), BF16),
        scratch_shapes=[
            pltpu.VMEM((s, tq), F32),
            pltpu.VMEM((s, 2 * tq), F32),
            pltpu.VMEM((s, tq), BF16),
        ],
        compiler_params=_params(3),
        name="diffattn",
    )(lam, dk, dqt, dvt, ng)


def _merge_kernel(x_ref, hm_ref, hd_ref, sg_ref, wa_ref, wb_ref, wo_ref, g_ref, wr_ref,
                  xo_ref, h2_ref, aff_ref):
    d = x_ref.shape[1]
    ya = jnp.dot(hm_ref[...], wa_ref[...], preferred_element_type=F32)
    yb = jnp.dot(hd_ref[...], wb_ref[...], preferred_element_type=F32)
    mix = sg_ref[:, :d].astype(F32) * ya + sg_ref[:, d:].astype(F32) * yb
    xn = x_ref[...] + jnp.dot(mix.astype(BF16), wo_ref[...], preferred_element_type=F32)
    xo_ref[...] = xn
    h2 = _rms(xn, g_ref[...])
    h2_ref[...] = h2.astype(BF16)
    logits = lax.dot_general(wr_ref[...], h2, NT_DIMS, precision=HI, preferred_element_type=F32)
    e = jnp.exp(logits - jnp.max(logits, axis=0, keepdims=True))
    aff_ref[...] = e / jnp.sum(e, axis=0, keepdims=True)


def _merge(x2, hm, hd, sgate, w_up_a, w_up_b, w_out, norm_g, w_router):
    t, d = x2.shape
    tm = min(512, t)
    wa = w_up_a.astype(BF16)
    wb = w_up_b.astype(BF16)
    wo = w_out.astype(BF16)
    wr = w_router.T

    def full(a):
        return pl.BlockSpec(a.shape, lambda i: (0, 0))

    def rows(n):
        return pl.BlockSpec((tm, n), lambda i: (i, 0))

    return pl.pallas_call(
        _merge_kernel,
        grid=(t // tm,),
        in_specs=[rows(d), rows(M_W), rows(D_W), rows(2 * d), full(wa), full(wb), full(wo),
                  full(norm_g), full(wr)],
        out_specs=(rows(d), rows(d), pl.BlockSpec((N_EXPERTS, tm), lambda i: (0, i))),
        out_shape=(jax.ShapeDtypeStruct((t, d), F32),
                   jax.ShapeDtypeStruct((t, d), BF16),
                   jax.ShapeDtypeStruct((N_EXPERTS, t), F32)),
        compiler_params=_params(1),
        name="merge",
    )(x2, hm, hd, sgate, wa, wb, wo, norm_g, wr)


def _route_kernel(aff_ref, prow_ref, pcol_ref, *, cap):
    aff = aff_ref[...]
    e, s = aff.shape
    bits = pltpu.bitcast(aff, jnp.int32)
    thr = jnp.zeros((e, 1), jnp.int32)
    for bit in range(30, -1, -1):
        cand = thr | jnp.int32(1 << bit)
        cnt = jnp.sum((bits >= cand).astype(F32), axis=1, keepdims=True)
        thr = jnp.where(cnt >= cap, cand, thr)
    gt = bits > thr
    eq = bits == thr
    need = cap - jnp.sum(gt.astype(F32), axis=1, keepdims=True)

    ri = lax.broadcasted_iota(jnp.int32, (LANES, LANES), 0)
    ci = lax.broadcasted_iota(jnp.int32, (LANES, LANES), 1)
    before = (ri < ci).astype(BF16)

    def excl_prefix(maskf):
        outs = []
        off = jnp.zeros((e, 1), F32)
        for blk in range(s // LANES):
            mb = maskf[:, blk * LANES:(blk + 1) * LANES]
            outs.append(jnp.dot(mb.astype(BF16), before, preferred_element_type=F32) + off)
            off = off + jnp.sum(mb, axis=1, keepdims=True)
        return jnp.concatenate(outs, axis=1)

    eqf = eq.astype(F32)
    rank_eq = excl_prefix(eqf)
    self = jnp.where(gt, 1.0, jnp.where(rank_eq < need, eqf, 0.0))
    pos = excl_prefix(self)
    prow = jnp.where(self > 0.5, pos, -1.0)
    prow_ref[...] = prow
    padded = jnp.concatenate([prow, jnp.full((LANES - e, s), -1.0, F32)], axis=0)
    pcol_ref[...] = padded.T


def _route(aff_t, b, s):
    cap = EC_CAPACITY * s // N_EXPERTS
    return pl.pallas_call(
        functools.partial(_route_kernel, cap=cap),
        grid=(b,),
        in_specs=[pl.BlockSpec((N_EXPERTS, s), lambda i: (0, i))],
        out_specs=(pl.BlockSpec((None, N_EXPERTS, s), lambda i: (i, 0, 0)),
                   pl.BlockSpec((None, s, LANES), lambda i: (i, 0, 0))),
        out_shape=(jax.ShapeDtypeStruct((b, N_EXPERTS, s), F32),
                   jax.ShapeDtypeStruct((b, s, LANES), F32)),
        compiler_params=_params(1),
        name="route",
    )(aff_t)


def _gather_kernel(prow_ref, aff_ref, h2_ref, xe_ref, g_ref, *, cap):
    for e in range(N_EXPERTS):
        prow = prow_ref[e:e + 1, :]
        slot = lax.broadcasted_iota(jnp.int32, (cap, prow.shape[1]), 0).astype(F32)
        hit = prow == slot
        onehot = jnp.where(hit, 1.0, 0.0).astype(BF16)
        xe_ref[e] = jnp.dot(onehot, h2_ref[...], preferred_element_type=F32).astype(BF16)
        g = jnp.sum(jnp.where(hit, aff_ref[e:e + 1, :], 0.0), axis=1, keepdims=True)
        g_ref[e] = jnp.broadcast_to(g, (cap, LANES))


def _gather(prow, aff_t, h2, b, s):
    cap = EC_CAPACITY * s // N_EXPERTS
    d = h2.shape[1]
    return pl.pallas_call(
        functools.partial(_gather_kernel, cap=cap),
        grid=(b,),
        in_specs=[pl.BlockSpec((None, N_EXPERTS, s), lambda i: (i, 0, 0)),
                  pl.BlockSpec((N_EXPERTS, s), lambda i: (0, i)),
                  pl.BlockSpec((s, d), lambda i: (i, 0))],
        out_specs=(pl.BlockSpec((N_EXPERTS, cap, d), lambda i: (0, i, 0)),
                   pl.BlockSpec((N_EXPERTS, cap, LANES), lambda i: (0, i, 0))),
        out_shape=(jax.ShapeDtypeStruct((N_EXPERTS, b * cap, d), BF16),
                   jax.ShapeDtypeStruct((N_EXPERTS, b * cap, LANES), F32)),
        compiler_params=_params(1),
        name="gather",
    )(prow, aff_t, h2)


def _ffn_kernel(xe_ref, g_ref, wg_ref, wu_ref, wd_ref, ye_ref, acc_ref):
    f = pl.program_id(2)
    xe = xe_ref[...]
    gate = jnp.dot(xe, wg_ref[...].astype(BF16), preferred_element_type=F32)
    up = jnp.dot(xe, wu_ref[...].astype(BF16), preferred_element_type=F32)
    hid = (gate * _sigmoid(gate) * up).astype(BF16)
    part = jnp.dot(hid, wd_ref[...].astype(BF16), preferred_element_type=F32)

    @pl.when(f == 0)
    def _():
        acc_ref[...] = part

    @pl.when(f > 0)
    def _():
        acc_ref[...] += part

    @pl.when(f == pl.num_programs(2) - 1)
    def _():
        g = g_ref[...]
        d = acc_ref.shape[1]
        for j in range(d // LANES):
            cs = slice(j * LANES, (j + 1) * LANES)
            ye_ref[:, cs] = (acc_ref[:, cs] * g).astype(BF16)


def _expert_ffn(xe, g, w_gate, w_up, w_down):
    e, m, d = xe.shape
    dff = w_gate.shape[2]
    tm = min(1024, m)
    tf = min(512, dff)
    return pl.pallas_call(
        _ffn_kernel,
        grid=(e, m // tm, dff // tf),
        in_specs=[pl.BlockSpec((None, tm, d), lambda ei, mi, fi: (ei, mi, 0)),
                  pl.BlockSpec((None, tm, LANES), lambda ei, mi, fi: (ei, mi, 0)),
                  pl.BlockSpec((None, d, tf), lambda ei, mi, fi: (ei, 0, fi)),
                  pl.BlockSpec((None, d, tf), lambda ei, mi, fi: (ei, 0, fi)),
                  pl.BlockSpec((None, tf, d), lambda ei, mi, fi: (ei, fi, 0))],
        out_specs=pl.BlockSpec((None, tm, d), lambda ei, mi, fi: (ei, mi, 0)),
        out_shape=jax.ShapeDtypeStruct((e, m, d), BF16),
        scratch_shapes=[pltpu.VMEM((tm, d), F32)],
        compiler_params=_params(3),
        name="expert_ffn",
    )(xe, g, w_gate, w_up, w_down)


def _scatter_kernel(x_ref, pcol_ref, ye_ref, ng_ref, out_ref, *, cap, final_norm):
    ts = x_ref.shape[0]
    slot = lax.broadcasted_iota(jnp.int32, (ts, cap), 1).astype(F32)
    acc = x_ref[...]
    for e in range(N_EXPERTS):
        onehot = jnp.where(pcol_ref[:, e:e + 1] == slot, 1.0, 0.0).astype(BF16)
        acc = acc + jnp.dot(onehot, ye_ref[e], preferred_element_type=F32)
    if final_norm:
        acc = _rms(acc, ng_ref[...])
    out_ref[...] = acc


def _scatter(x2, pcol, ye, norm_g, b, s, final_norm):
    t, d = x2.shape
    cap = EC_CAPACITY * s // N_EXPERTS
    ts = min(512, s)
    ns = s // ts
    pcol2 = pcol.reshape(t, LANES)
    return pl.pallas_call(
        functools.partial(_scatter_kernel, cap=cap, final_norm=final_norm),
        grid=(b, ns),
        in_specs=[pl.BlockSpec((ts, d), lambda i, j: (i * ns + j, 0)),
                  pl.BlockSpec((ts, LANES), lambda i, j: (i * ns + j, 0)),
                  pl.BlockSpec((N_EXPERTS, cap, d), lambda i, j: (0, i, 0)),
                  pl.BlockSpec(norm_g.shape, lambda i, j: (0, 0))],
        out_specs=pl.BlockSpec((ts, d), lambda i, j: (i * ns + j, 0)),
        out_shape=jax.ShapeDtypeStruct((t, d), F32),
        compiler_params=_params(2),
        name="scatter",
    )(x2, pcol2, ye, norm_g)


def kernel(x, norm_mix_g, w_in, b_mgate, conv_w, conv_b, mlstm_norm_g, diff_lam, diff_norm_g,
           w_up_a, w_up_b, w_out, norm_ffn_g, w_router, w_gate_e, w_up_e, w_down_e, norm_f_g):
    b, s, d = x.shape
    depth = w_in.shape[0]
    nc = s // CHUNK
    t = b * s
    x2 = x.reshape(t, d)
    for l in range(depth):
        lam_init = 0.8 - 0.6 * math.exp(-0.3 * l)
        zqk, zv, so, gcol, dk, sgate, dqt, dvt, grow = _in_projection(
            x2, norm_mix_g[l].reshape(1, d), w_in[l])
        grow = grow[:N_MGATES].reshape(N_MGATES, b, nc, CHUNK).transpose(1, 2, 0, 3)
        hm = _mlstm_branch(zqk.reshape(b, s, -1), zv.reshape(b, s, -1), so.reshape(b, s, -1),
                           gcol.reshape(b, s, -1), grow, conv_w[l], conv_b[l], b_mgate[l],
                           mlstm_norm_g[l])
        hd = _diff_attention(dk.reshape(b, s, -1), dqt, dvt, diff_lam[l], diff_norm_g[l],
                             lam_init, b, s)
        x2, h2, aff_t = _merge(x2, hm.reshape(t, -1), hd.reshape(t, -1), sgate, w_up_a[l], w_up_b[l],
                               w_out[l], norm_ffn_g[l].reshape(1, d), w_router[l])
        prow, pcol = _route(aff_t, b, s)
        xe, g = _gather(prow, aff_t, h2, b, s)
        ye = _expert_ffn(xe, g, w_gate_e[l], w_up_e[l], w_down_e[l])
        x2 = _scatter(x2, pcol, ye, norm_f_g.reshape(1, d), b, s, final_norm=(l == depth - 1))
    return x2.reshape(b, s, d)
```

```python
import functools
import math

import jax
import jax.numpy as jnp
from jax import lax
from jax.experimental import pallas as pl
from jax.experimental.pallas import tpu as pltpu

F32 = jnp.float32
BF16 = jnp.bfloat16
EPS = 1e-6

LANES = 128
VMEM_LIMIT_BYTES = 56 * 1024 * 1024

M_HEADS = 4
M_DH = 128
M_W = M_HEADS * M_DH
CHUNK = 128
CONV_TAPS = 5
N_MGATES = 4 * M_HEADS
D_HEADS = 4
D_DH = 64
D_DV = 2 * D_DH
D_W = D_HEADS * D_DV
N_EXPERTS = 16
EC_CAPACITY = 2

LOG2E = 1.4426950408889634
HI = lax.Precision.HIGHEST
NT_DIMS = (((1,), (1,)), ((), ()))
TN_DIMS = (((0,), (0,)), ((), ()))


def _params(n_axes):
    return pltpu.CompilerParams(dimension_semantics=("arbitrary",) * n_axes,
                                vmem_limit_bytes=VMEM_LIMIT_BYTES)


def _sigmoid(v):
    return 1.0 / (1.0 + jnp.exp(-v))


def _log_sigmoid(v):
    return jnp.minimum(v, 0.0) - jnp.log1p(jnp.exp(-jnp.abs(v)))


def _rms(v, g):
    return v * lax.rsqrt(jnp.mean(v * v, axis=-1, keepdims=True) + EPS) * g


def _inproj_kernel(x_ref, g_ref, wqk_ref, wv_ref, wo_ref, wg_ref, wdk_ref, wgate_ref, wt_ref,
                   zqk_ref, zv_ref, so_ref, gc_ref, dk_ref, sgate_ref, dqt_ref, dvt_ref, gr_ref):
    hb = _rms(x_ref[...], g_ref[...]).astype(BF16)

    def proj(w_ref):
        return jnp.dot(hb, w_ref[...], preferred_element_type=F32)

    zqk_ref[...] = proj(wqk_ref).astype(BF16)
    zv_ref[...] = proj(wv_ref).astype(BF16)
    so_ref[...] = _sigmoid(proj(wo_ref)).astype(BF16)
    gc_ref[...] = proj(wg_ref)
    dk_ref[...] = proj(wdk_ref).astype(BF16)
    sgate_ref[...] = _sigmoid(proj(wgate_ref)).astype(BF16)
    zt = lax.dot_general(wt_ref[...], hb, NT_DIMS, preferred_element_type=F32)
    dqt_ref[...] = zt[:D_W].astype(BF16)
    dvt_ref[...] = zt[D_W:2 * D_W].astype(BF16)
    gr_ref[...] = zt[2 * D_W:]


def _in_projection(x2, norm_g, w_in):
    t, d = x2.shape
    tm = min(512, t)
    o_mq, o_mv, o_mo, o_mg = 0, 2 * M_W, 3 * M_W, 4 * M_W
    o_dq = o_mg + N_MGATES
    o_dk = o_dq + D_W
    o_dv = o_dk + D_W
    o_gate = o_dv + D_W
    w_g = w_in[:, o_mg:o_dq]
    wqk = w_in[:, o_mq:o_mv].astype(BF16)
    wv = w_in[:, o_mv:o_mo].astype(BF16)
    wo = w_in[:, o_mo:o_mg].astype(BF16)
    wg = jnp.pad(w_g, ((0, 0), (0, LANES - N_MGATES))).astype(BF16)
    wdk = w_in[:, o_dk:o_dv].astype(BF16)
    wgate = w_in[:, o_gate:].astype(BF16)
    wt = jnp.concatenate([
        (w_in[:, o_dq:o_dk] * (D_DH ** -0.5 * LOG2E)).T,
        w_in[:, o_dv:o_gate].T,
        jnp.pad(w_g, ((0, 0), (0, LANES - N_MGATES))).T], axis=0).astype(BF16)
    weights = (wqk, wv, wo, wg, wdk, wgate, wt)

    def full(a):
        return pl.BlockSpec(a.shape, lambda i: (0, 0))

    def rows(n):
        return pl.BlockSpec((tm, n), lambda i: (i, 0))

    def cols(n):
        return pl.BlockSpec((n, tm), lambda i: (0, i))

    out_shape = (
        jax.ShapeDtypeStruct((t, 2 * M_W), BF16),
        jax.ShapeDtypeStruct((t, M_W), BF16),
        jax.ShapeDtypeStruct((t, M_W), BF16),
        jax.ShapeDtypeStruct((t, LANES), F32),
        jax.ShapeDtypeStruct((t, D_W), BF16),
        jax.ShapeDtypeStruct((t, 2 * d), BF16),
        jax.ShapeDtypeStruct((D_W, t), BF16),
        jax.ShapeDtypeStruct((D_W, t), BF16),
        jax.ShapeDtypeStruct((LANES, t), F32),
    )
    out_specs = (rows(2 * M_W), rows(M_W), rows(M_W), rows(LANES), rows(D_W), rows(2 * d),
                 cols(D_W), cols(D_W), cols(LANES))
    return pl.pallas_call(
        _inproj_kernel,
        grid=(t // tm,),
        in_specs=[rows(d), full(norm_g)] + [full(w) for w in weights],
        out_specs=out_specs,
        out_shape=out_shape,
        compiler_params=_params(1),
        name="inproj",
    )(x2, norm_g, *weights)


def _mlstm_kernel(zq_ref, zk_ref, zv_ref, so_ref, gc_ref, gr_ref, cwq_ref, cwk_ref, cbq_ref, cbk_ref,
                  bgr_ref, bgc_ref, ng_ref,
                  out_ref,
                  xpad_ref, q_ref, k_ref, gcs_ref, cumc_ref, grs_ref, cumr_ref, hdir_ref):
    head = pl.program_id(1)
    s = zv_ref.shape[0]
    nc = s // CHUNK
    L = CHUNK
    pad = 8
    rc = min(256, s)

    ri = lax.broadcasted_iota(jnp.int32, (L, L), 0)
    ci = lax.broadcasted_iota(jnp.int32, (L, L), 1)
    lower = (ci <= ri)
    upper = (ci >= ri)

    @pl.when(head == 0)
    def _():
        tril = lower.astype(F32)
        triu = upper.astype(F32)
        grow = gr_ref[...] + bgc_ref[...]
        grs_ref[...] = grow
        lsr = _log_sigmoid(grow).reshape(nc * N_MGATES, L)
        cumr_ref[0] = jnp.dot(lsr, triu, precision=HI, preferred_element_type=F32).reshape(nc, N_MGATES, L)
        cumr_ref[1] = jnp.dot(lsr, tril, precision=HI, preferred_element_type=F32).reshape(nc, N_MGATES, L)

        def cum_cols(c, carry):
            r = pl.ds(pl.multiple_of(c * L, L), L)
            gcol = gc_ref[r, :N_MGATES] + bgr_ref[...]
            gcs_ref[r, :] = gcol
            lsc = _log_sigmoid(gcol)
            cumc_ref[0, r, :] = jnp.dot(tril, lsc, precision=HI, preferred_element_type=F32)
            cumc_ref[1, r, :] = jnp.dot(triu, lsc, precision=HI, preferred_element_type=F32)
            return carry

        lax.fori_loop(0, nc, cum_cols, 0)

    xpad_ref[0:pad, :] = jnp.zeros((pad, LANES), F32)
    xpad_ref[pad + s:2 * pad + s, :] = jnp.zeros((pad, LANES), F32)
    half = CONV_TAPS // 2
    for z_ref, cw_ref, cb_ref, dst_ref, scale in ((zq_ref, cwq_ref, cbq_ref, q_ref, None),
                                                  (zk_ref, cwk_ref, cbk_ref, k_ref, M_DH ** -0.5)):
        for c0 in range(0, s, rc):
            xpad_ref[pad + c0:pad + c0 + rc, :] = z_ref[c0:c0 + rc, :].astype(F32)
        for c0 in range(0, s, rc):
            acc = cb_ref[...]
            for tap in range(CONV_TAPS):
                o = pad - half + tap + c0
                acc = acc + cw_ref[tap:tap + 1, :] * xpad_ref[o:o + rc, :]
            y = acc * _sigmoid(acc)
            if scale is not None:
                y = y * scale
            dst_ref[c0:c0 + rc, :] = y.astype(BF16)

    neg_inf = jnp.float32(-jnp.inf)

    def chunk_step(h, d, c, state):
        ct_in, n_in, m_in = state
        r = pl.ds(pl.multiple_of(c * L, L), L)
        fcol = d * 2 * M_HEADS + M_HEADS + h
        icol = d * 2 * M_HEADS + h
        q = q_ref[r, :]
        k = k_ref[r, :]
        v = zv_ref[r, :]
        b_col = cumc_ref[d, r, fcol:fcol + 1]
        u_col = gcs_ref[r, icol:icol + 1] - b_col
        b_row = cumr_ref[d, c, fcol:fcol + 1, :]
        u_row = grs_ref[c, icol:icol + 1, :] - b_row
        g = b_row[:, L - 1:L] if d == 0 else b_row[:, 0:1]
        a_max = g + jnp.max(u_row, axis=1, keepdims=True)
        w_col = jnp.exp(g + u_col - a_max)
        vw = (v.astype(F32) * w_col).astype(BF16)
        ct_loc = lax.dot_general(k, vw, TN_DIMS, preferred_element_type=F32)
        n_loc = jnp.sum(k.astype(F32) * w_col, axis=0, keepdims=True)
        dm = jnp.where(lower if d == 0 else upper, b_col + u_row, neg_inf)
        inter = b_col + m_in
        m_t = jnp.maximum(inter, jnp.max(dm, axis=1, keepdims=True))
        p = jnp.exp(dm - m_t)
        w = lax.dot_general(q, k, NT_DIMS, preferred_element_type=F32) * p
        s_in = jnp.exp(inter - m_t)
        num = (jnp.dot(w.astype(BF16), v, preferred_element_type=F32)
               + s_in * jnp.dot(q, ct_in.astype(BF16), preferred_element_type=F32))
        den = (jnp.sum(w, axis=1, keepdims=True)
               + s_in * jnp.sum(q.astype(F32) * n_in, axis=1, keepdims=True))
        hdir_ref[d, r, :] = num / jnp.maximum(jnp.abs(den), jnp.exp(-m_t))
        m_new = jnp.maximum(g + m_in, a_max)
        sp = jnp.exp(g + m_in - m_new)
        sl = jnp.exp(a_max - m_new)
        return (sp * ct_in + sl * ct_loc, sp * n_in + sl * n_loc, m_new)

    for h in range(M_HEADS):
        @pl.when(head == h)
        def _(h=h):
            def body(c, carry):
                sf, sb = carry
                sf = chunk_step(h, 0, c, sf)
                sb = chunk_step(h, 1, nc - 1 - c, sb)
                return (sf, sb)

            zero = (jnp.zeros((M_DH, M_DH), F32), jnp.zeros((1, M_DH), F32), jnp.zeros((1, 1), F32))
            lax.fori_loop(0, nc, body, (zero, zero))

    def finish(c, carry):
        r = pl.ds(pl.multiple_of(c * rc, rc), rc)
        hs = hdir_ref[0, r, :] + hdir_ref[1, r, :]
        mu = jnp.mean(hs, axis=-1, keepdims=True)
        hc = hs - mu
        var = jnp.mean(hc * hc, axis=-1, keepdims=True)
        hn = hc * lax.rsqrt(var + EPS) * ng_ref[...] * so_ref[r, :].astype(F32)
        out_ref[r, :] = hn.astype(BF16)
        return carry

    lax.fori_loop(0, s // rc, finish, 0)


def _mlstm_branch(zqk, zv, so, gcol, grow, conv_w, conv_b, b_mgate, norm_g):
    b, s, _ = zv.shape
    nc = s // CHUNK
    bgr = b_mgate.reshape(1, N_MGATES)
    bgc = b_mgate.reshape(N_MGATES, 1)
    cb = conv_b.reshape(1, -1)
    ng = norm_g.reshape(1, -1)

    def head_block(off):
        return pl.BlockSpec((None, s, M_DH), lambda i, h: (i, 0, h + off))

    def head_cols(rows, off):
        return pl.BlockSpec((rows, M_DH), lambda i, h: (0, h + off))

    def full(a):
        return pl.BlockSpec(a.shape, lambda i, h: (0,) * a.ndim)

    return pl.pallas_call(
        _mlstm_kernel,
        grid=(b, M_HEADS),
        in_specs=[
            head_block(0), head_block(M_HEADS),
            head_block(0), head_block(0),
            pl.BlockSpec((None, s, LANES), lambda i, h: (i, 0, 0)),
            pl.BlockSpec((None, nc, N_MGATES, CHUNK), lambda i, h: (i, 0, 0, 0)),
            head_cols(CONV_TAPS, 0), head_cols(CONV_TAPS, M_HEADS),
            head_cols(1, 0), head_cols(1, M_HEADS),
            full(bgr), full(bgc), head_cols(1, 0),
        ],
        out_specs=head_block(0),
        out_shape=jax.ShapeDtypeStruct((b, s, M_W), BF16),
        scratch_shapes=[
            pltpu.VMEM((s + 16, LANES), F32),
            pltpu.VMEM((s, M_DH), BF16),
            pltpu.VMEM((s, M_DH), BF16),
            pltpu.VMEM((s, N_MGATES), F32),
            pltpu.VMEM((2, s, N_MGATES), F32),
            pltpu.VMEM((nc, N_MGATES, CHUNK), F32),
            pltpu.VMEM((2, nc, N_MGATES, CHUNK), F32),
            pltpu.VMEM((2, s, M_DH), F32),
        ],
        compiler_params=_params(2),
        name="mlstm",
    )(zqk, zqk, zv, so, gcol, grow, conv_w, conv_w, cb, cb, bgr, bgc, ng)


def _diff_kernel(lam_ref, k_ref, qt_ref, vt_ref, ng_ref, out_ref, bias_ref, s_ref, p_ref,
                 *, tq, tk, lam_init):
    h = pl.program_id(0)
    qi = pl.program_id(1)
    b = pl.program_id(2)
    s = k_ref.shape[0]
    nk = s // tk

    @pl.when(b == 0)
    def _():
        slope = jnp.float32(2.0 ** (-8.0 * D_HEADS / D_HEADS))
        for i in range(D_HEADS - 1):
            slope = jnp.where(h == i, jnp.float32(2.0 ** (-8.0 * (i + 1) / D_HEADS)), slope)
        kpos = lax.broadcasted_iota(jnp.int32, (s, tq), 0)
        qpos = lax.broadcasted_iota(jnp.int32, (s, tq), 1) + qi * tq
        bias_ref[...] = (slope * LOG2E) * jnp.abs(qpos - kpos).astype(F32)

    lam = lam_ref[...]
    lam_full = (jnp.exp(jnp.sum(lam[0:1] * lam[1:2], axis=1, keepdims=True))
                - jnp.exp(jnp.sum(lam[2:3] * lam[3:4], axis=1, keepdims=True)) + lam_init)

    qt = qt_ref[...].astype(F32)
    rows = lax.broadcasted_iota(jnp.int32, qt.shape, 0)
    qblk = jnp.concatenate([jnp.where(rows < D_DH, qt, 0.0), jnp.where(rows >= D_DH, qt, 0.0)],
                           axis=1).astype(BF16)

    m = jnp.full((1, 2 * tq), -jnp.inf, F32)
    for c in range(nk):
        r = slice(c * tk, (c + 1) * tk)
        sc = jnp.dot(k_ref[r, :], qblk, preferred_element_type=F32)
        bias = bias_ref[r, :]
        sc0 = sc[:, :tq] - bias
        sc1 = sc[:, tq:] - bias
        s_ref[r, :tq] = sc0
        s_ref[r, tq:] = sc1
        m = jnp.maximum(m, jnp.concatenate([jnp.max(sc0, axis=0, keepdims=True),
                                            jnp.max(sc1, axis=0, keepdims=True)], axis=1))

    l = jnp.zeros((1, 2 * tq), F32)
    for c in range(nk):
        r = slice(c * tk, (c + 1) * tk)
        p = jnp.exp2(s_ref[r, :] - m)
        l = l + jnp.sum(p, axis=0, keepdims=True)
        p_ref[r, :] = p.astype(BF16)

    ot = jnp.dot(vt_ref[...], p_ref[...], preferred_element_type=F32)
    ot = ot[:, :tq] * (1.0 / l[:, :tq]) - ot[:, tq:] * (lam_full / l[:, tq:])
    ot = ot * lax.rsqrt(jnp.mean(ot * ot, axis=0, keepdims=True) + EPS)
    ot = ot * ng_ref[...] * (1.0 - lam_init)
    out_ref[...] = ot.T.astype(BF16)


def _diff_attention(dk, dqt, dvt, lam, norm_g, lam_init, b, s):
    tq = min(256, s)
    tk = min(256, s)
    nq = s // tq
    ng = norm_g.reshape(D_DV, 1)
    kern = functools.partial(_diff_kernel, tq=tq, tk=tk, lam_init=lam_init)
    return pl.pallas_call(
        kern,
        grid=(D_HEADS, nq, b),
        in_specs=[
            pl.BlockSpec(lam.shape, lambda h, q, i: (0, 0)),
            pl.BlockSpec((None, s, D_DV), lambda h, q, i: (i, 0, h)),
            pl.BlockSpec((D_DV, tq), lambda h, q, i: (h, i * nq + q)),
            pl.BlockSpec((D_DV, s), lambda h, q, i: (h, i)),
            pl.BlockSpec(ng.shape, lambda h, q, i: (0, 0)),
        ],
        out_specs=pl.BlockSpec((None, tq, D_DV), lambda h, q, i: (i, q, h)),
        out_shape=jax.ShapeDtypeStruct((b, s, D_W), BF16),
        scratch_shapes=[
            pltpu.VMEM((s, tq), F32),
            pltpu.VMEM((s, 2 * tq), F32),
            pltpu.VMEM((s, 2 * tq), BF16),
        ],
        compiler_params=_params(3),
        name="diffattn",
    )(lam, dk, dqt, dvt, ng)


def _merge_kernel(x_ref, hm_ref, hd_ref, sg_ref, wa_ref, wb_ref, wo_ref, g_ref, wr_ref,
                  xo_ref, h2_ref, aff_ref):
    d = x_ref.shape[1]
    ya = jnp.dot(hm_ref[...], wa_ref[...], preferred_element_type=F32)
    yb = jnp.dot(hd_ref[...], wb_ref[...], preferred_element_type=F32)
    mix = sg_ref[:, :d].astype(F32) * ya + sg_ref[:, d:].astype(F32) * yb
    xn = x_ref[...] + jnp.dot(mix.astype(BF16), wo_ref[...], preferred_element_type=F32)
    xo_ref[...] = xn
    h2 = _rms(xn, g_ref[...])
    h2_ref[...] = h2.astype(BF16)
    logits = lax.dot_general(wr_ref[...], h2, NT_DIMS, precision=HI, preferred_element_type=F32)
    e = jnp.exp(logits - jnp.max(logits, axis=0, keepdims=True))
    aff_ref[...] = e / jnp.sum(e, axis=0, keepdims=True)


def _merge(x2, hm, hd, sgate, w_up_a, w_up_b, w_out, norm_g, w_router):
    t, d = x2.shape
    tm = min(512, t)
    wa = w_up_a.astype(BF16)
    wb = w_up_b.astype(BF16)
    wo = w_out.astype(BF16)
    wr = w_router.T

    def full(a):
        return pl.BlockSpec(a.shape, lambda i: (0, 0))

    def rows(n):
        return pl.BlockSpec((tm, n), lambda i: (i, 0))

    return pl.pallas_call(
        _merge_kernel,
        grid=(t // tm,),
        in_specs=[rows(d), rows(M_W), rows(D_W), rows(2 * d), full(wa), full(wb), full(wo),
                  full(norm_g), full(wr)],
        out_specs=(rows(d), rows(d), pl.BlockSpec((N_EXPERTS, tm), lambda i: (0, i))),
        out_shape=(jax.ShapeDtypeStruct((t, d), F32),
                   jax.ShapeDtypeStruct((t, d), BF16),
                   jax.ShapeDtypeStruct((N_EXPERTS, t), F32)),
        compiler_params=_params(1),
        name="merge",
    )(x2, hm, hd, sgate, wa, wb, wo, norm_g, wr)


def _route_kernel(aff_ref, prow_ref, pcol_ref, *, cap):
    aff = aff_ref[...]
    e, s = aff.shape
    bits = pltpu.bitcast(aff, jnp.int32)
    thr = jnp.zeros((e, 1), jnp.int32)
    for bit in range(30, -1, -1):
        cand = thr | jnp.int32(1 << bit)
        cnt = jnp.sum((bits >= cand).astype(F32), axis=1, keepdims=True)
        thr = jnp.where(cnt >= cap, cand, thr)
    gt = bits > thr
    eq = bits == thr
    need = cap - jnp.sum(gt.astype(F32), axis=1, keepdims=True)

    ri = lax.broadcasted_iota(jnp.int32, (LANES, LANES), 0)
    ci = lax.broadcasted_iota(jnp.int32, (LANES, LANES), 1)
    before = (ri < ci).astype(BF16)

    def excl_prefix(maskf):
        outs = []
        off = jnp.zeros((e, 1), F32)
        for blk in range(s // LANES):
            mb = maskf[:, blk * LANES:(blk + 1) * LANES]
            outs.append(jnp.dot(mb.astype(BF16), before, preferred_element_type=F32) + off)
            off = off + jnp.sum(mb, axis=1, keepdims=True)
        return jnp.concatenate(outs, axis=1)

    eqf = eq.astype(F32)
    rank_eq = excl_prefix(eqf)
    self = jnp.where(gt, 1.0, jnp.where(rank_eq < need, eqf, 0.0))
    pos = excl_prefix(self)
    prow = jnp.where(self > 0.5, pos, -1.0)
    prow_ref[...] = prow
    padded = jnp.concatenate([prow, jnp.full((LANES - e, s), -1.0, F32)], axis=0)
    pcol_ref[...] = padded.T


def _route(aff_t, b, s):
    cap = EC_CAPACITY * s // N_EXPERTS
    return pl.pallas_call(
        functools.partial(_route_kernel, cap=cap),
        grid=(b,),
        in_specs=[pl.BlockSpec((N_EXPERTS, s), lambda i: (0, i))],
        out_specs=(pl.BlockSpec((None, N_EXPERTS, s), lambda i: (i, 0, 0)),
                   pl.BlockSpec((None, s, LANES), lambda i: (i, 0, 0))),
        out_shape=(jax.ShapeDtypeStruct((b, N_EXPERTS, s), F32),
                   jax.ShapeDtypeStruct((b, s, LANES), F32)),
        compiler_params=_params(1),
        name="route",
    )(aff_t)


def _gather_kernel(prow_ref, aff_ref, h2_ref, xe_ref, g_ref, *, cap):
    for e in range(N_EXPERTS):
        prow = prow_ref[e:e + 1, :]
        slot = lax.broadcasted_iota(jnp.int32, (cap, prow.shape[1]), 0).astype(F32)
        hit = prow == slot
        onehot = jnp.where(hit, 1.0, 0.0).astype(BF16)
        xe_ref[e] = jnp.dot(onehot, h2_ref[...], preferred_element_type=F32).astype(BF16)
        g = jnp.sum(jnp.where(hit, aff_ref[e:e + 1, :], 0.0), axis=1, keepdims=True)
        g_ref[e] = jnp.broadcast_to(g, (cap, LANES))


def _gather(prow, aff_t, h2, b, s):
    cap = EC_CAPACITY * s // N_EXPERTS
    d = h2.shape[1]
    return pl.pallas_call(
        functools.partial(_gather_kernel, cap=cap),
        grid=(b,),
        in_specs=[pl.BlockSpec((None, N_EXPERTS, s), lambda i: (i, 0, 0)),
                  pl.BlockSpec((N_EXPERTS, s), lambda i: (0, i)),
                  pl.BlockSpec((s, d), lambda i: (i, 0))],
        out_specs=(pl.BlockSpec((N_EXPERTS, cap, d), lambda i: (0, i, 0)),
                   pl.BlockSpec((N_EXPERTS, cap, LANES), lambda i: (0, i, 0))),
        out_shape=(jax.ShapeDtypeStruct((N_EXPERTS, b * cap, d), BF16),
                   jax.ShapeDtypeStruct((N_EXPERTS, b * cap, LANES), F32)),
        compiler_params=_params(1),
        name="gather",
    )(prow, aff_t, h2)


def _ffn_kernel(xe_ref, g_ref, wg_ref, wu_ref, wd_ref, ye_ref, acc_ref):
    f = pl.program_id(2)
    xe = xe_ref[...]
    gate = jnp.dot(xe, wg_ref[...].astype(BF16), preferred_element_type=F32)
    up = jnp.dot(xe, wu_ref[...].astype(BF16), preferred_element_type=F32)
    hid = (gate * _sigmoid(gate) * up).astype(BF16)
    part = jnp.dot(hid, wd_ref[...].astype(BF16), preferred_element_type=F32)

    @pl.when(f == 0)
    def _():
        acc_ref[...] = part

    @pl.when(f > 0)
    def _():
        acc_ref[...] += part

    @pl.when(f == pl.num_programs(2) - 1)
    def _():
        g = g_ref[...]
        d = acc_ref.shape[1]
        for j in range(d // LANES):
            cs = slice(j * LANES, (j + 1) * LANES)
            ye_ref[:, cs] = (acc_ref[:, cs] * g).astype(BF16)


def _expert_ffn(xe, g, w_gate, w_up, w_down, layer):
    e, m, d = xe.shape
    dff = w_gate.shape[3]
    tm = min(1024, m)
    tf = min(512, dff)
    return pl.pallas_call(
        _ffn_kernel,
        grid=(e, m // tm, dff // tf),
        in_specs=[pl.BlockSpec((None, tm, d), lambda ei, mi, fi: (ei, mi, 0)),
                  pl.BlockSpec((None, tm, LANES), lambda ei, mi, fi: (ei, mi, 0)),
                  pl.BlockSpec((None, None, d, tf), lambda ei, mi, fi: (layer, ei, 0, fi)),
                  pl.BlockSpec((None, None, d, tf), lambda ei, mi, fi: (layer, ei, 0, fi)),
                  pl.BlockSpec((None, None, tf, d), lambda ei, mi, fi: (layer, ei, fi, 0))],
        out_specs=pl.BlockSpec((None, tm, d), lambda ei, mi, fi: (ei, mi, 0)),
        out_shape=jax.ShapeDtypeStruct((e, m, d), BF16),
        scratch_shapes=[pltpu.VMEM((tm, d), F32)],
        compiler_params=_params(3),
        name="expert_ffn",
    )(xe, g, w_gate, w_up, w_down)


def _scatter_kernel(x_ref, pcol_ref, ye_ref, ng_ref, out_ref, *, cap, final_norm):
    ts = x_ref.shape[0]
    slot = lax.broadcasted_iota(jnp.int32, (ts, cap), 1).astype(F32)
    acc = x_ref[...]
    for e in range(N_EXPERTS):
        onehot = jnp.where(pcol_ref[:, e:e + 1] == slot, 1.0, 0.0).astype(BF16)
        acc = acc + jnp.dot(onehot, ye_ref[e], preferred_element_type=F32)
    if final_norm:
        acc = _rms(acc, ng_ref[...])
    out_ref[...] = acc


def _scatter(x2, pcol, ye, norm_g, b, s, final_norm):
    t, d = x2.shape
    cap = EC_CAPACITY * s // N_EXPERTS
    ts = min(512, s)
    ns = s // ts
    pcol2 = pcol.reshape(t, LANES)
    return pl.pallas_call(
        functools.partial(_scatter_kernel, cap=cap, final_norm=final_norm),
        grid=(b, ns),
        in_specs=[pl.BlockSpec((ts, d), lambda i, j: (i * ns + j, 0)),
                  pl.BlockSpec((ts, LANES), lambda i, j: (i * ns + j, 0)),
                  pl.BlockSpec((N_EXPERTS, cap, d), lambda i, j: (0, i, 0)),
                  pl.BlockSpec(norm_g.shape, lambda i, j: (0, 0))],
        out_specs=pl.BlockSpec((ts, d), lambda i, j: (i * ns + j, 0)),
        out_shape=jax.ShapeDtypeStruct((t, d), F32),
        compiler_params=_params(2),
        name="scatter",
    )(x2, pcol2, ye, norm_g)


def kernel(x, norm_mix_g, w_in, b_mgate, conv_w, conv_b, mlstm_norm_g, diff_lam, diff_norm_g,
           w_up_a, w_up_b, w_out, norm_ffn_g, w_router, w_gate_e, w_up_e, w_down_e, norm_f_g):
    b, s, d = x.shape
    depth = w_in.shape[0]
    nc = s // CHUNK
    t = b * s
    x2 = x.reshape(t, d)
    for l in range(depth):
        lam_init = 0.8 - 0.6 * math.exp(-0.3 * l)
        zqk, zv, so, gcol, dk, sgate, dqt, dvt, grow = _in_projection(
            x2, norm_mix_g[l].reshape(1, d), w_in[l])
        grow = grow[:N_MGATES].reshape(N_MGATES, b, nc, CHUNK).transpose(1, 2, 0, 3)
        hm = _mlstm_branch(zqk.reshape(b, s, -1), zv.reshape(b, s, -1), so.reshape(b, s, -1),
                           gcol.reshape(b, s, -1), grow, conv_w[l], conv_b[l], b_mgate[l],
                           mlstm_norm_g[l])
        hd = _diff_attention(dk.reshape(b, s, -1), dqt, dvt, diff_lam[l], diff_norm_g[l],
                             lam_init, b, s)
        x2, h2, aff_t = _merge(x2, hm.reshape(t, -1), hd.reshape(t, -1), sgate, w_up_a[l], w_up_b[l],
                               w_out[l], norm_ffn_g[l].reshape(1, d), w_router[l])
        prow, pcol = _route(aff_t, b, s)
        xe, g = _gather(prow, aff_t, h2, b, s)
        ye = _expert_ffn(xe, g, w_gate_e, w_up_e, w_down_e, l)
        x2 = _scatter(x2, pcol, ye, norm_f_g.reshape(1, d), b, s, final_norm=(l == depth - 1))
    return x2.reshape(b, s, d)
```

```python
import functools
import math

import jax
import jax.numpy as jnp
from jax import lax
from jax.experimental import pallas as pl
from jax.experimental.pallas import tpu as pltpu

F32 = jnp.float32
BF16 = jnp.bfloat16
EPS = 1e-6

LANES = 128
VMEM_LIMIT_BYTES = 56 * 1024 * 1024

M_HEADS = 4
M_DH = 128
M_W = M_HEADS * M_DH
CHUNK = 128
CONV_TAPS = 5
N_MGATES = 4 * M_HEADS
D_HEADS = 4
D_DH = 64
D_DV = 2 * D_DH
D_W = D_HEADS * D_DV
N_EXPERTS = 16
EC_CAPACITY = 2

LOG2E = 1.4426950408889634
HI = lax.Precision.HIGHEST
NT_DIMS = (((1,), (1,)), ((), ()))
TN_DIMS = (((0,), (0,)), ((), ()))


def _params(n_axes):
    return pltpu.CompilerParams(dimension_semantics=("arbitrary",) * n_axes,
                                vmem_limit_bytes=VMEM_LIMIT_BYTES)


def _sigmoid(v):
    return 1.0 / (1.0 + jnp.exp(-v))


def _log_sigmoid(v):
    return jnp.minimum(v, 0.0) - jnp.log1p(jnp.exp(-jnp.abs(v)))


def _rms(v, g):
    return v * lax.rsqrt(jnp.mean(v * v, axis=-1, keepdims=True) + EPS) * g


def _inproj_kernel(x_ref, g_ref, wqk_ref, wo_ref, wg_ref, wdk_ref, wgate_ref, wt_ref,
                   zqk_ref, so_ref, gc_ref, dk_ref, sgate_ref, dqt_ref, dvt_ref, mvt_ref, gr_ref):
    hb = _rms(x_ref[...], g_ref[...]).astype(BF16)

    def proj(w_ref):
        return jnp.dot(hb, w_ref[...], preferred_element_type=F32)

    zqk_ref[...] = proj(wqk_ref).astype(BF16)
    so_ref[...] = _sigmoid(proj(wo_ref)).astype(BF16)
    gc_ref[...] = proj(wg_ref)
    dk_ref[...] = proj(wdk_ref).astype(BF16)
    sgate_ref[...] = _sigmoid(proj(wgate_ref)).astype(BF16)
    zt = lax.dot_general(wt_ref[...], hb, NT_DIMS, preferred_element_type=F32)
    dqt_ref[...] = zt[:D_W].astype(BF16)
    dvt_ref[...] = zt[D_W:2 * D_W].astype(BF16)
    mvt_ref[...] = zt[2 * D_W:2 * D_W + M_W].astype(BF16)
    gr_ref[...] = zt[2 * D_W + M_W:]


def _paired_forget_cols():
    return jnp.array([(j // (2 * M_HEADS)) * 2 * M_HEADS + M_HEADS + j % M_HEADS for j in range(N_MGATES)],
                     jnp.int32)


def _in_projection(x2, norm_g, w_in):
    t, d = x2.shape
    tm = min(512, t)
    o_mq, o_mv, o_mo, o_mg = 0, 2 * M_W, 3 * M_W, 4 * M_W
    o_dq = o_mg + N_MGATES
    o_dk = o_dq + D_W
    o_dv = o_dk + D_W
    o_gate = o_dv + D_W
    w_g = w_in[:, o_mg:o_dq]
    wqk = w_in[:, o_mq:o_mv].astype(BF16)
    wo = w_in[:, o_mo:o_mg].astype(BF16)
    wg = jnp.concatenate([jnp.pad(w_g, ((0, 0), (0, LANES - N_MGATES))),
                          jnp.pad(w_g[:, _paired_forget_cols()], ((0, 0), (0, LANES - N_MGATES)))],
                         axis=1).astype(BF16)
    wdk = w_in[:, o_dk:o_dv].astype(BF16)
    wgate = w_in[:, o_gate:].astype(BF16)
    wt = jnp.concatenate([
        (w_in[:, o_dq:o_dk] * (D_DH ** -0.5 * LOG2E)).T,
        w_in[:, o_dv:o_gate].T,
        w_in[:, o_mv:o_mo].T,
        jnp.pad(w_g, ((0, 0), (0, LANES - N_MGATES))).T], axis=0).astype(BF16)
    weights = (wqk, wo, wg, wdk, wgate, wt)

    def full(a):
        return pl.BlockSpec(a.shape, lambda i: (0, 0))

    def rows(n):
        return pl.BlockSpec((tm, n), lambda i: (i, 0))

    def cols(n):
        return pl.BlockSpec((n, tm), lambda i: (0, i))

    out_shape = (
        jax.ShapeDtypeStruct((t, 2 * M_W), BF16),
        jax.ShapeDtypeStruct((t, M_W), BF16),
        jax.ShapeDtypeStruct((t, 2 * LANES), F32),
        jax.ShapeDtypeStruct((t, D_W), BF16),
        jax.ShapeDtypeStruct((t, 2 * d), BF16),
        jax.ShapeDtypeStruct((D_W, t), BF16),
        jax.ShapeDtypeStruct((D_W, t), BF16),
        jax.ShapeDtypeStruct((M_W, t), BF16),
        jax.ShapeDtypeStruct((LANES, t), F32),
    )
    out_specs = (rows(2 * M_W), rows(M_W), rows(2 * LANES), rows(D_W), rows(2 * d),
                 cols(D_W), cols(D_W), cols(M_W), cols(LANES))
    return pl.pallas_call(
        _inproj_kernel,
        grid=(t // tm,),
        in_specs=[rows(d), full(norm_g)] + [full(w) for w in weights],
        out_specs=out_specs,
        out_shape=out_shape,
        compiler_params=_params(1),
        name="inproj",
    )(x2, norm_g, *weights)


ONES_ROWS = 16
STAT_ROWS = 8
R_MIN, R_SP, R_SL = range(3)
R_SIN, R_CLAMP, R_DEN = range(3)


def _mlstm_kernel(zq_ref, zk_ref, vt_ref, so_ref, gc_ref, gr_ref, cwq_ref, cwk_ref, cbq_ref, cbk_ref,
                  bgr_ref, bgp_ref, bgc_ref, ng_ref,
                  out_ref,
                  xpad_ref, qt_ref, k_ref, va_ref, ucol_ref, grs_ref, cumr_ref,
                  mst_ref, lst_ref, st_ref, wbf_ref, vaw_ref, cnloc_ref, cnin_ref, hdir_ref,
                  *, unroll):
    head = pl.program_id(1)
    s = zq_ref.shape[0]
    nc = s // CHUNK
    L = CHUNK
    pad = 8
    rc = min(256, s)

    ri = lax.broadcasted_iota(jnp.int32, (L, L), 0)
    ci = lax.broadcasted_iota(jnp.int32, (L, L), 1)
    le = (ri <= ci)
    ge = (ri >= ci)

    @pl.when(head == 0)
    def _():
        tril = ge.astype(F32)
        triu = le.astype(F32)
        grow = gr_ref[...] + bgc_ref[...]
        grs_ref[...] = grow
        lsr = _log_sigmoid(grow).reshape(nc * N_MGATES, L)
        cumr_ref[0] = jnp.dot(lsr, triu, precision=HI, preferred_element_type=F32).reshape(nc, N_MGATES, L)
        cumr_ref[1] = jnp.dot(lsr, tril, precision=HI, preferred_element_type=F32).reshape(nc, N_MGATES, L)

        def u_cols(c, carry):
            r = pl.ds(pl.multiple_of(c * L, L), L)
            gin = gc_ref[r, :N_MGATES] + bgr_ref[...]
            lsf = _log_sigmoid(gc_ref[r, LANES:LANES + N_MGATES] + bgp_ref[...])
            ucol_ref[0, r, :] = gin - jnp.dot(tril, lsf, precision=HI, preferred_element_type=F32)
            ucol_ref[1, r, :] = gin - jnp.dot(triu, lsf, precision=HI, preferred_element_type=F32)
            return carry

        lax.fori_loop(0, nc, u_cols, 0, unroll=2)

    xpad_ref[0:pad, :] = jnp.zeros((pad, LANES), F32)
    xpad_ref[pad + s:2 * pad + s, :] = jnp.zeros((pad, LANES), F32)
    half = CONV_TAPS // 2
    for z_ref, cw_ref, cb_ref, is_q in ((zq_ref, cwq_ref, cbq_ref, True), (zk_ref, cwk_ref, cbk_ref, False)):
        for c0 in range(0, s, rc):
            xpad_ref[pad + c0:pad + c0 + rc, :] = z_ref[c0:c0 + rc, :].astype(F32)
        for c0 in range(0, s, rc):
            acc = cb_ref[...]
            for tap in range(CONV_TAPS):
                o = pad - half + tap + c0
                acc = acc + cw_ref[tap:tap + 1, :] * xpad_ref[o:o + rc, :]
            y = acc * _sigmoid(acc)
            if is_q:
                for j in range(rc // L):
                    qt_ref[(c0 // L) + j] = y[j * L:(j + 1) * L, :].T.astype(BF16)
            else:
                k_ref[c0:c0 + rc, :] = (y * (M_DH ** -0.5)).astype(BF16)

    ones_blk = jnp.where(lax.broadcasted_iota(jnp.int32, (ONES_ROWS, L), 0) == 0, 1.0, 0.0).astype(BF16)
    for c in range(nc):
        va_ref[c, 0:M_DH, :] = vt_ref[:, c * L:(c + 1) * L]
        va_ref[c, M_DH:M_DH + ONES_ROWS, :] = ones_blk

    neg_inf = jnp.float32(-jnp.inf)

    def gate_rows(h, d, c):
        fcol = d * 2 * M_HEADS + M_HEADS + h
        icol = d * 2 * M_HEADS + h
        b_row = cumr_ref[d, c, fcol:fcol + 1, :]
        u_row = grs_ref[c, icol:icol + 1, :] - b_row
        g = b_row[:, L - 1:L] if d == 0 else b_row[:, 0:1]
        return b_row, u_row, g

    def stabiliser_scan(h, d):
        m = jnp.zeros((1, 1), F32)
        for i in range(nc):
            c = i if d == 0 else nc - 1 - i
            _, u_row, g = gate_rows(h, d, c)
            a_max = g + jnp.max(u_row, axis=1, keepdims=True)
            m_new = jnp.maximum(g + m, a_max)
            mst_ref[d, c, R_MIN:R_MIN + 1, :] = jnp.broadcast_to(m, (1, L))
            mst_ref[d, c, R_SP:R_SP + 1, :] = jnp.broadcast_to(jnp.exp(g + m - m_new), (1, L))
            mst_ref[d, c, R_SL:R_SL + 1, :] = jnp.broadcast_to(jnp.exp(a_max - m_new), (1, L))
            m = m_new

    def score_step(c):
        r = pl.ds(pl.multiple_of(c * L, L), L)
        st_ref[c] = jnp.dot(k_ref[r, :], qt_ref[c], preferred_element_type=F32)

    def decay_step(h, d, c):
        r = pl.ds(pl.multiple_of(c * L, L), L)
        icol = d * 2 * M_HEADS + h
        b_row, u_row, g = gate_rows(h, d, c)
        u_col = ucol_ref[d, r, icol:icol + 1]
        a_max = g + jnp.max(u_row, axis=1, keepdims=True)
        w_row = jnp.exp(g + u_row - a_max)
        vaw_ref[d, c] = (va_ref[c].astype(F32) * w_row).astype(BF16)
        dm = jnp.where(le if d == 0 else ge, u_col + b_row, neg_inf)
        inter = b_row + mst_ref[d, c, R_MIN:R_MIN + 1, :]
        m_t = jnp.maximum(inter, jnp.max(dm, axis=0, keepdims=True))
        w = st_ref[c] * jnp.exp(dm - m_t)
        wbf_ref[d, c] = w.astype(BF16)
        lst_ref[d, c, R_SIN:R_SIN + 1, :] = jnp.exp(inter - m_t)
        lst_ref[d, c, R_CLAMP:R_CLAMP + 1, :] = jnp.exp(-m_t)
        lst_ref[d, c, R_DEN:R_DEN + 1, :] = jnp.sum(w, axis=0, keepdims=True)

    def matmul_step(d, c):
        r = pl.ds(pl.multiple_of(c * L, L), L)
        cnloc_ref[d, c] = jnp.dot(vaw_ref[d, c], k_ref[r, :], preferred_element_type=F32)
        hdir_ref[d, c] = jnp.dot(va_ref[c, 0:M_DH, :], wbf_ref[d, c], preferred_element_type=F32)

    def state_scan(d):
        def body(i, cn):
            c = i if d == 0 else nc - 1 - i
            cnin_ref[d, c] = cn.astype(BF16)
            return (mst_ref[d, c, R_SP:R_SP + 1, :] * cn + mst_ref[d, c, R_SL:R_SL + 1, :] * cnloc_ref[d, c])

        lax.fori_loop(0, nc, body, jnp.zeros((M_DH + ONES_ROWS, M_DH), F32), unroll=2)

    def direction_output(d, c):
        cq = jnp.dot(cnin_ref[d, c], qt_ref[c], preferred_element_type=F32)
        s_in = lst_ref[d, c, R_SIN:R_SIN + 1, :]
        num = hdir_ref[d, c] + s_in * cq[:M_DH, :]
        den = lst_ref[d, c, R_DEN:R_DEN + 1, :] + s_in * cq[M_DH:M_DH + 1, :]
        return num * (1.0 / jnp.maximum(jnp.abs(den), lst_ref[d, c, R_CLAMP:R_CLAMP + 1, :]))

    def score_body(i, carry):
        for j in range(unroll):
            score_step(i * unroll + j)
        return carry

    lax.fori_loop(0, nc // unroll, score_body, 0)

    for h in range(M_HEADS):
        @pl.when(head == h)
        def _(h=h):
            stabiliser_scan(h, 0)
            stabiliser_scan(h, 1)

            def decay_body(i, carry):
                for j in range(unroll // 2):
                    decay_step(h, 0, i * (unroll // 2) + j)
                    decay_step(h, 1, i * (unroll // 2) + j)
                return carry

            lax.fori_loop(0, nc // (unroll // 2), decay_body, 0)

    def matmul_body(i, carry):
        for j in range(unroll):
            matmul_step(0, i * unroll + j)
            matmul_step(1, i * unroll + j)
        return carry

    lax.fori_loop(0, nc // unroll, matmul_body, 0)
    state_scan(0)
    state_scan(1)

    def finish(i, carry):
        for j in range(unroll):
            c = i * unroll + j
            r = pl.ds(pl.multiple_of(c * L, L), L)
            hs = direction_output(0, c) + direction_output(1, c)
            mu = jnp.mean(hs, axis=0, keepdims=True)
            hc = hs - mu
            var = jnp.mean(hc * hc, axis=0, keepdims=True)
            hn = (hc * lax.rsqrt(var + EPS)).T
            out_ref[r, :] = (hn * ng_ref[...] * so_ref[r, :].astype(F32)).astype(BF16)
        return carry

    lax.fori_loop(0, nc // unroll, finish, 0)


def _mlstm_branch(zqk, mvt, so, gcol, grow, conv_w, conv_b, b_mgate, norm_g):
    b, s, _ = so.shape
    nc = s // CHUNK
    unroll = 8 if nc % 8 == 0 else (4 if nc % 4 == 0 else 1)
    bgr = b_mgate.reshape(1, N_MGATES)
    bgp = b_mgate[_paired_forget_cols()].reshape(1, N_MGATES)
    bgc = b_mgate.reshape(N_MGATES, 1)
    cb = conv_b.reshape(1, -1)
    ng = norm_g.reshape(1, -1)

    def head_block(off):
        return pl.BlockSpec((None, s, M_DH), lambda i, h: (i, 0, h + off))

    def head_cols(rows, off):
        return pl.BlockSpec((rows, M_DH), lambda i, h: (0, h + off))

    def full(a):
        return pl.BlockSpec(a.shape, lambda i, h: (0,) * a.ndim)

    return pl.pallas_call(
        functools.partial(_mlstm_kernel, unroll=unroll),
        grid=(b, M_HEADS),
        in_specs=[
            head_block(0), head_block(M_HEADS),
            pl.BlockSpec((M_DH, s), lambda i, h: (h, i)),
            head_block(0),
            pl.BlockSpec((None, s, 2 * LANES), lambda i, h: (i, 0, 0)),
            pl.BlockSpec((None, nc, N_MGATES, CHUNK), lambda i, h: (i, 0, 0, 0)),
            head_cols(CONV_TAPS, 0), head_cols(CONV_TAPS, M_HEADS),
            head_cols(1, 0), head_cols(1, M_HEADS),
            full(bgr), full(bgp), full(bgc), head_cols(1, 0),
        ],
        out_specs=head_block(0),
        out_shape=jax.ShapeDtypeStruct((b, s, M_W), BF16),
        scratch_shapes=[
            pltpu.VMEM((s + 16, LANES), F32),
            pltpu.VMEM((nc, M_DH, CHUNK), BF16),
            pltpu.VMEM((s, M_DH), BF16),
            pltpu.VMEM((nc, M_DH + ONES_ROWS, CHUNK), BF16),
            pltpu.VMEM((2, s, N_MGATES), F32),
            pltpu.VMEM((nc, N_MGATES, CHUNK), F32),
            pltpu.VMEM((2, nc, N_MGATES, CHUNK), F32),
            pltpu.VMEM((2, nc, STAT_ROWS, CHUNK), F32),
            pltpu.VMEM((2, nc, STAT_ROWS, CHUNK), F32),
            pltpu.VMEM((nc, CHUNK, CHUNK), F32),
            pltpu.VMEM((2, nc, CHUNK, CHUNK), BF16),
            pltpu.VMEM((2, nc, M_DH + ONES_ROWS, CHUNK), BF16),
            pltpu.VMEM((2, nc, M_DH + ONES_ROWS, M_DH), F32),
            pltpu.VMEM((2, nc, M_DH + ONES_ROWS, M_DH), BF16),
            pltpu.VMEM((2, nc, M_DH, CHUNK), F32),
        ],
        compiler_params=_params(2),
        name="mlstm",
    )(zqk, zqk, mvt, so, gcol, grow, conv_w, conv_w, cb, cb, bgr, bgp, bgc, ng)


def _diff_kernel(lam_ref, k_ref, qt_ref, vt_ref, ng_ref, out_ref, bias_ref, s_ref, p_ref,
                 *, tq, tk, lam_init):
    h = pl.program_id(0)
    qi = pl.program_id(1)
    b = pl.program_id(2)
    s = k_ref.shape[0]
    nk = s // tk

    @pl.when(b == 0)
    def _():
        slope = jnp.float32(2.0 ** (-8.0 * D_HEADS / D_HEADS))
        for i in range(D_HEADS - 1):
            slope = jnp.where(h == i, jnp.float32(2.0 ** (-8.0 * (i + 1) / D_HEADS)), slope)
        kpos = lax.broadcasted_iota(jnp.int32, (s, tq), 0)
        qpos = lax.broadcasted_iota(jnp.int32, (s, tq), 1) + qi * tq
        bias_ref[...] = (slope * LOG2E) * jnp.abs(qpos - kpos).astype(F32)

    lam = lam_ref[...]
    lam_full = (jnp.exp(jnp.sum(lam[0:1] * lam[1:2], axis=1, keepdims=True))
                - jnp.exp(jnp.sum(lam[2:3] * lam[3:4], axis=1, keepdims=True)) + lam_init)

    qt = qt_ref[...].astype(F32)
    rows = lax.broadcasted_iota(jnp.int32, qt.shape, 0)
    qblk = jnp.concatenate([jnp.where(rows < D_DH, qt, 0.0), jnp.where(rows >= D_DH, qt, 0.0)],
                           axis=1).astype(BF16)

    m = jnp.full((1, 2 * tq), -jnp.inf, F32)
    for c in range(nk):
        r = slice(c * tk, (c + 1) * tk)
        sc = jnp.dot(k_ref[r, :], qblk, preferred_element_type=F32)
        bias = bias_ref[r, :]
        sc0 = sc[:, :tq] - bias
        sc1 = sc[:, tq:] - bias
        s_ref[r, :tq] = sc0
        s_ref[r, tq:] = sc1
        m = jnp.maximum(m, jnp.concatenate([jnp.max(sc0, axis=0, keepdims=True),
                                            jnp.max(sc1, axis=0, keepdims=True)], axis=1))

    l = jnp.zeros((1, 2 * tq), F32)
    for c in range(nk):
        r = slice(c * tk, (c + 1) * tk)
        p = jnp.exp2(s_ref[r, :] - m)
        l = l + jnp.sum(p, axis=0, keepdims=True)
        p_ref[r, :] = p.astype(BF16)

    ot = jnp.dot(vt_ref[...], p_ref[...], preferred_element_type=F32)
    ot = ot[:, :tq] * (1.0 / l[:, :tq]) - ot[:, tq:] * (lam_full / l[:, tq:])
    ot = ot * lax.rsqrt(jnp.mean(ot * ot, axis=0, keepdims=True) + EPS)
    ot = ot * ng_ref[...] * (1.0 - lam_init)
    out_ref[...] = ot.T.astype(BF16)


def _diff_attention(dk, dqt, dvt, lam, norm_g, lam_init, b, s):
    tq = min(256, s)
    tk = min(256, s)
    nq = s // tq
    ng = norm_g.reshape(D_DV, 1)
    kern = functools.partial(_diff_kernel, tq=tq, tk=tk, lam_init=lam_init)
    return pl.pallas_call(
        kern,
        grid=(D_HEADS, nq, b),
        in_specs=[
            pl.BlockSpec(lam.shape, lambda h, q, i: (0, 0)),
            pl.BlockSpec((None, s, D_DV), lambda h, q, i: (i, 0, h)),
            pl.BlockSpec((D_DV, tq), lambda h, q, i: (h, i * nq + q)),
            pl.BlockSpec((D_DV, s), lambda h, q, i: (h, i)),
            pl.BlockSpec(ng.shape, lambda h, q, i: (0, 0)),
        ],
        out_specs=pl.BlockSpec((None, tq, D_DV), lambda h, q, i: (i, q, h)),
        out_shape=jax.ShapeDtypeStruct((b, s, D_W), BF16),
        scratch_shapes=[
            pltpu.VMEM((s, tq), F32),
            pltpu.VMEM((s, 2 * tq), F32),
            pltpu.VMEM((s, 2 * tq), BF16),
        ],
        compiler_params=_params(3),
        name="diffattn",
    )(lam, dk, dqt, dvt, ng)


def _merge_kernel(x_ref, hm_ref, hd_ref, sg_ref, wa_ref, wb_ref, wo_ref, g_ref, wr_ref,
                  xo_ref, h2_ref, aff_ref):
    d = x_ref.shape[1]
    ya = jnp.dot(hm_ref[...], wa_ref[...], preferred_element_type=F32)
    yb = jnp.dot(hd_ref[...], wb_ref[...], preferred_element_type=F32)
    mix = sg_ref[:, :d].astype(F32) * ya + sg_ref[:, d:].astype(F32) * yb
    xn = x_ref[...] + jnp.dot(mix.astype(BF16), wo_ref[...], preferred_element_type=F32)
    xo_ref[...] = xn
    h2 = _rms(xn, g_ref[...])
    h2_ref[...] = h2.astype(BF16)
    logits = lax.dot_general(wr_ref[...], h2, NT_DIMS, precision=HI, preferred_element_type=F32)
    e = jnp.exp(logits - jnp.max(logits, axis=0, keepdims=True))
    aff_ref[...] = e / jnp.sum(e, axis=0, keepdims=True)


def _merge(x2, hm, hd, sgate, w_up_a, w_up_b, w_out, norm_g, w_router):
    t, d = x2.shape
    tm = min(512, t)
    wa = w_up_a.astype(BF16)
    wb = w_up_b.astype(BF16)
    wo = w_out.astype(BF16)
    wr = w_router.T

    def full(a):
        return pl.BlockSpec(a.shape, lambda i: (0, 0))

    def rows(n):
        return pl.BlockSpec((tm, n), lambda i: (i, 0))

    return pl.pallas_call(
        _merge_kernel,
        grid=(t // tm,),
        in_specs=[rows(d), rows(M_W), rows(D_W), rows(2 * d), full(wa), full(wb), full(wo),
                  full(norm_g), full(wr)],
        out_specs=(rows(d), rows(d), pl.BlockSpec((N_EXPERTS, tm), lambda i: (0, i))),
        out_shape=(jax.ShapeDtypeStruct((t, d), F32),
                   jax.ShapeDtypeStruct((t, d), BF16),
                   jax.ShapeDtypeStruct((N_EXPERTS, t), F32)),
        compiler_params=_params(1),
        name="merge",
    )(x2, hm, hd, sgate, wa, wb, wo, norm_g, wr)


def _route_kernel(aff_ref, prow_ref, pcol_ref, *, cap):
    aff = aff_ref[...]
    e, s = aff.shape
    bits = pltpu.bitcast(aff, jnp.int32)
    thr = jnp.zeros((e, 1), jnp.int32)
    for bit in range(30, -1, -1):
        cand = thr | jnp.int32(1 << bit)
        cnt = jnp.sum((bits >= cand).astype(F32), axis=1, keepdims=True)
        thr = jnp.where(cnt >= cap, cand, thr)
    gt = bits > thr
    eq = bits == thr
    need = cap - jnp.sum(gt.astype(F32), axis=1, keepdims=True)

    ri = lax.broadcasted_iota(jnp.int32, (LANES, LANES), 0)
    ci = lax.broadcasted_iota(jnp.int32, (LANES, LANES), 1)
    before = (ri < ci).astype(BF16)

    def excl_prefix(maskf):
        outs = []
        off = jnp.zeros((e, 1), F32)
        for blk in range(s // LANES):
            mb = maskf[:, blk * LANES:(blk + 1) * LANES]
            outs.append(jnp.dot(mb.astype(BF16), before, preferred_element_type=F32) + off)
            off = off + jnp.sum(mb, axis=1, keepdims=True)
        return jnp.concatenate(outs, axis=1)

    eqf = eq.astype(F32)
    rank_eq = excl_prefix(eqf)
    self = jnp.where(gt, 1.0, jnp.where(rank_eq < need, eqf, 0.0))
    pos = excl_prefix(self)
    prow = jnp.where(self > 0.5, pos, -1.0)
    prow_ref[...] = prow
    padded = jnp.concatenate([prow, jnp.full((LANES - e, s), -1.0, F32)], axis=0)
    pcol_ref[...] = padded.T


def _route(aff_t, b, s):
    cap = EC_CAPACITY * s // N_EXPERTS
    return pl.pallas_call(
        functools.partial(_route_kernel, cap=cap),
        grid=(b,),
        in_specs=[pl.BlockSpec((N_EXPERTS, s), lambda i: (0, i))],
        out_specs=(pl.BlockSpec((None, N_EXPERTS, s), lambda i: (i, 0, 0)),
                   pl.BlockSpec((None, s, LANES), lambda i: (i, 0, 0))),
        out_shape=(jax.ShapeDtypeStruct((b, N_EXPERTS, s), F32),
                   jax.ShapeDtypeStruct((b, s, LANES), F32)),
        compiler_params=_params(1),
        name="route",
    )(aff_t)


def _gather_kernel(prow_ref, aff_ref, h2_ref, xe_ref, g_ref, *, cap):
    for e in range(N_EXPERTS):
        prow = prow_ref[e:e + 1, :]
        slot = lax.broadcasted_iota(jnp.int32, (cap, prow.shape[1]), 0).astype(F32)
        hit = prow == slot
        onehot = jnp.where(hit, 1.0, 0.0).astype(BF16)
        xe_ref[e] = jnp.dot(onehot, h2_ref[...], preferred_element_type=F32).astype(BF16)
        g = jnp.sum(jnp.where(hit, aff_ref[e:e + 1, :], 0.0), axis=1, keepdims=True)
        g_ref[e] = jnp.broadcast_to(g, (cap, LANES))


def _gather(prow, aff_t, h2, b, s):
    cap = EC_CAPACITY * s // N_EXPERTS
    d = h2.shape[1]
    return pl.pallas_call(
        functools.partial(_gather_kernel, cap=cap),
        grid=(b,),
        in_specs=[pl.BlockSpec((None, N_EXPERTS, s), lambda i: (i, 0, 0)),
                  pl.BlockSpec((N_EXPERTS, s), lambda i: (0, i)),
                  pl.BlockSpec((s, d), lambda i: (i, 0))],
        out_specs=(pl.BlockSpec((N_EXPERTS, cap, d), lambda i: (0, i, 0)),
                   pl.BlockSpec((N_EXPERTS, cap, LANES), lambda i: (0, i, 0))),
        out_shape=(jax.ShapeDtypeStruct((N_EXPERTS, b * cap, d), BF16),
                   jax.ShapeDtypeStruct((N_EXPERTS, b * cap, LANES), F32)),
        compiler_params=_params(1),
        name="gather",
    )(prow, aff_t, h2)


def _ffn_kernel(xe_ref, g_ref, wg_ref, wu_ref, wd_ref, ye_ref, acc_ref):
    f = pl.program_id(2)
    xe = xe_ref[...]
    gate = jnp.dot(xe, wg_ref[...].astype(BF16), preferred_element_type=F32)
    up = jnp.dot(xe, wu_ref[...].astype(BF16), preferred_element_type=F32)
    hid = (gate * _sigmoid(gate) * up).astype(BF16)
    part = jnp.dot(hid, wd_ref[...].astype(BF16), preferred_element_type=F32)

    @pl.when(f == 0)
    def _():
        acc_ref[...] = part

    @pl.when(f > 0)
    def _():
        acc_ref[...] += part

    @pl.when(f == pl.num_programs(2) - 1)
    def _():
        g = g_ref[...]
        d = acc_ref.shape[1]
        for j in range(d // LANES):
            cs = slice(j * LANES, (j + 1) * LANES)
            ye_ref[:, cs] = (acc_ref[:, cs] * g).astype(BF16)


def _expert_ffn(xe, g, w_gate, w_up, w_down, layer):
    e, m, d = xe.shape
    dff = w_gate.shape[3]
    tm = min(1024, m)
    tf = min(512, dff)
    return pl.pallas_call(
        _ffn_kernel,
        grid=(e, m // tm, dff // tf),
        in_specs=[pl.BlockSpec((None, tm, d), lambda ei, mi, fi: (ei, mi, 0)),
                  pl.BlockSpec((None, tm, LANES), lambda ei, mi, fi: (ei, mi, 0)),
                  pl.BlockSpec((None, None, d, tf), lambda ei, mi, fi: (layer, ei, 0, fi)),
                  pl.BlockSpec((None, None, d, tf), lambda ei, mi, fi: (layer, ei, 0, fi)),
                  pl.BlockSpec((None, None, tf, d), lambda ei, mi, fi: (layer, ei, fi, 0))],
        out_specs=pl.BlockSpec((None, tm, d), lambda ei, mi, fi: (ei, mi, 0)),
        out_shape=jax.ShapeDtypeStruct((e, m, d), BF16),
        scratch_shapes=[pltpu.VMEM((tm, d), F32)],
        compiler_params=_params(3),
        name="expert_ffn",
    )(xe, g, w_gate, w_up, w_down)


def _scatter_kernel(x_ref, pcol_ref, ye_ref, ng_ref, out_ref, *, cap, final_norm):
    ts = x_ref.shape[0]
    slot = lax.broadcasted_iota(jnp.int32, (ts, cap), 1).astype(F32)
    acc = x_ref[...]
    for e in range(N_EXPERTS):
        onehot = jnp.where(pcol_ref[:, e:e + 1] == slot, 1.0, 0.0).astype(BF16)
        acc = acc + jnp.dot(onehot, ye_ref[e], preferred_element_type=F32)
    if final_norm:
        acc = _rms(acc, ng_ref[...])
    out_ref[...] = acc


def _scatter(x2, pcol, ye, norm_g, b, s, final_norm):
    t, d = x2.shape
    cap = EC_CAPACITY * s // N_EXPERTS
    ts = min(512, s)
    ns = s // ts
    pcol2 = pcol.reshape(t, LANES)
    return pl.pallas_call(
        functools.partial(_scatter_kernel, cap=cap, final_norm=final_norm),
        grid=(b, ns),
        in_specs=[pl.BlockSpec((ts, d), lambda i, j: (i * ns + j, 0)),
                  pl.BlockSpec((ts, LANES), lambda i, j: (i * ns + j, 0)),
                  pl.BlockSpec((N_EXPERTS, cap, d), lambda i, j: (0, i, 0)),
                  pl.BlockSpec(norm_g.shape, lambda i, j: (0, 0))],
        out_specs=pl.BlockSpec((ts, d), lambda i, j: (i * ns + j, 0)),
        out_shape=jax.ShapeDtypeStruct((t, d), F32),
        compiler_params=_params(2),
        name="scatter",
    )(x2, pcol2, ye, norm_g)


def kernel(x, norm_mix_g, w_in, b_mgate, conv_w, conv_b, mlstm_norm_g, diff_lam, diff_norm_g,
           w_up_a, w_up_b, w_out, norm_ffn_g, w_router, w_gate_e, w_up_e, w_down_e, norm_f_g):
    b, s, d = x.shape
    depth = w_in.shape[0]
    nc = s // CHUNK
    t = b * s
    x2 = x.reshape(t, d)
    for l in range(depth):
        lam_init = 0.8 - 0.6 * math.exp(-0.3 * l)
        zqk, so, gcol, dk, sgate, dqt, dvt, mvt, grow = _in_projection(
            x2, norm_mix_g[l].reshape(1, d), w_in[l])
        grow = grow[:N_MGATES].reshape(N_MGATES, b, nc, CHUNK).transpose(1, 2, 0, 3)
        hm = _mlstm_branch(zqk.reshape(b, s, -1), mvt, so.reshape(b, s, -1),
                           gcol.reshape(b, s, -1), grow, conv_w[l], conv_b[l], b_mgate[l],
                           mlstm_norm_g[l])
        hd = _diff_attention(dk.reshape(b, s, -1), dqt, dvt, diff_lam[l], diff_norm_g[l],
                             lam_init, b, s)
        x2, h2, aff_t = _merge(x2, hm.reshape(t, -1), hd.reshape(t, -1), sgate, w_up_a[l], w_up_b[l],
                               w_out[l], norm_ffn_g[l].reshape(1, d), w_router[l])
        prow, pcol = _route(aff_t, b, s)
        xe, g = _gather(prow, aff_t, h2, b, s)
        ye = _expert_ffn(xe, g, w_gate_e, w_up_e, w_down_e, l)
        x2 = _scatter(x2, pcol, ye, norm_f_g.reshape(1, d), b, s, final_norm=(l == depth - 1))
    return x2.reshape(b, s, d)
```
